```python
import math
import jax, jax.numpy as jnp
from jax import lax
import numpy as np

D_MODEL = 1024
BATCH = 8
SEQ = 2048
DEPTH = 1

ATTN_WIDTH = D_MODEL // 2
N_ATTN_HEADS = 8
HEAD_DIM = ATTN_WIDTH // N_ATTN_HEADS
LRU_WIDTH = D_MODEL - ATTN_WIDTH
N_LRU_BLOCKS = 8
LRU_BLOCK = LRU_WIDTH // N_LRU_BLOCKS
CONV_WIDTH = 4
LRU_C = 8.0
MOBA_BLOCK = 256
MOBA_TOPK = 3
Q_CHUNK = 16
D_FF = 4 * D_MODEL
IN_WIDTH = 3 * ATTN_WIDTH + 2 * LRU_WIDTH
LN_EPS = 1e-5
DEEPNORM_ALPHA = (2.0 * DEPTH) ** 0.25
DEEPNORM_BETA = (8.0 * DEPTH) ** -0.25
NEG_INF = -1e30

kernel_name = "hymba_moba_rglru_deepnorm_layer"


def layer_norm(x, g, b):
    xf = x.astype(jnp.float32)
    mu = jnp.mean(xf, axis=-1, keepdims=True)
    var = jnp.mean(jnp.square(xf - mu), axis=-1, keepdims=True)
    y = (xf - mu) * lax.rsqrt(var + LN_EPS) * g.astype(jnp.float32) + b.astype(jnp.float32)
    return y.astype(x.dtype)


def alibi_slopes(n_heads):
    h = jnp.arange(1, n_heads + 1, dtype=jnp.float32)
    return jnp.exp2(-8.0 * h / n_heads)


def moba_attention(q, k, v):
    B, S, H, Dh = q.shape
    nb = -(-S // MOBA_BLOCK)
    sp = nb * MOBA_BLOCK
    pad = sp - S

    def prep(t):
        t = jnp.pad(t, ((0, 0), (0, pad), (0, 0), (0, 0)))
        return t.transpose(0, 2, 1, 3)

    q, k, v = prep(q), prep(k), prep(v)
    kb = k.reshape(B, H, nb, MOBA_BLOCK, Dh)
    vb = v.reshape(B, H, nb, MOBA_BLOCK, Dh)
    n_chunks = sp // Q_CHUNK
    topk = min(MOBA_TOPK, nb - 1)
    q_blk = jnp.arange(sp) // MOBA_BLOCK

    if topk > 0:
        k_mean = jnp.mean(kb.astype(jnp.float32), axis=3)
        gate = jnp.einsum('bhtd,bhnd->bhtn', q.astype(jnp.float32), k_mean)
        fully_past = jnp.arange(nb)[None, :] < q_blk[:, None]
        gate = jnp.where(fully_past, gate, NEG_INF)
        _, sel = lax.top_k(gate, topk)
        sel = sel.astype(jnp.int32)
    else:
        sel = jnp.zeros((B, H, sp, 0), jnp.int32)
    sel_c = sel.reshape(B, H, n_chunks, Q_CHUNK, topk).transpose(2, 0, 1, 3, 4)
    q_c = q.reshape(B, H, n_chunks, Q_CHUNK, Dh).transpose(2, 0, 1, 3, 4)

    slopes = alibi_slopes(H)[None, :, None, None]
    bi = jnp.arange(B)[:, None, None, None]
    hi = jnp.arange(H)[None, :, None, None]
    key_off = jnp.arange(MOBA_BLOCK)
    scale = Dh ** -0.5

    def chunk_fn(args):
        c, qc, selc = args
        t = c * Q_CHUNK + jnp.arange(Q_CHUNK)
        own = (c * Q_CHUNK) // MOBA_BLOCK
        k_own = lax.dynamic_index_in_dim(kb, own, axis=2, keepdims=False)
        v_own = lax.dynamic_index_in_dim(vb, own, axis=2, keepdims=False)
        s_own = own * MOBA_BLOCK + key_off
        lo = jnp.einsum('bhqd,bhcd->bhqc', qc, k_own).astype(jnp.float32) * scale
        lo = lo - slopes * (t[:, None] - s_own[None, :])
        lo = jnp.where(s_own[None, :] <= t[:, None], lo, NEG_INF)
        kg = kb[bi, hi, selc]
        vg = vb[bi, hi, selc]
        s_past = selc[..., None] * MOBA_BLOCK + key_off
        lp = jnp.einsum('bhqd,bhqkcd->bhqkc', qc, kg).astype(jnp.float32) * scale
        lp = lp - slopes[..., None] * (t[:, None, None] - s_past)
        lp = jnp.where((selc < own)[..., None], lp, NEG_INF)
        n_past = topk * MOBA_BLOCK
        lp = lp.reshape(B, H, Q_CHUNK, n_past)
        p = jax.nn.softmax(jnp.concatenate([lp, lo], axis=-1), axis=-1).astype(v.dtype)
        out = jnp.einsum('bhqn,bhqnd->bhqd', p[..., :n_past],
                         vg.reshape(B, H, Q_CHUNK, n_past, Dh))
        out = out + jnp.einsum('bhqc,bhcd->bhqd', p[..., n_past:], v_own)
        return out

    out = lax.map(chunk_fn, (jnp.arange(n_chunks, dtype=jnp.int32), q_c, sel_c))
    out = out.transpose(1, 0, 3, 2, 4).reshape(B, sp, H * Dh)
    return out[:, :S]


def causal_depthwise_conv(x, w, b):
    S = x.shape[1]
    xp = jnp.pad(x, ((0, 0), (CONV_WIDTH - 1, 0), (0, 0)))
    y = b
    for kk in range(CONV_WIDTH):
        y = y + xp[:, kk:kk + S] * w[kk]
    return y


def block_diag_linear(x, w, b):
    xb = x.reshape(x.shape[0], x.shape[1], N_LRU_BLOCKS, LRU_BLOCK)
    y = jnp.einsum('bsnd,nde->bsne', xb, w).reshape(x.shape)
    return y + b


def rg_lru(x, w_rg, b_rg, w_ig, b_ig, lam):
    xf = x.astype(jnp.float32)
    r = jax.nn.sigmoid(block_diag_linear(x, w_rg, b_rg).astype(jnp.float32))
    i = jax.nn.sigmoid(block_diag_linear(x, w_ig, b_ig).astype(jnp.float32))
    log_a = -LRU_C * r * jax.nn.softplus(-lam.astype(jnp.float32))
    a = jnp.exp(log_a)
    u = jnp.sqrt(-jnp.expm1(2.0 * log_a)) * (i * xf)

    def combine(left, right):
        a1, b1 = left
        a2, b2 = right
        return a1 * a2, a2 * b1 + b2

    _, h = lax.associative_scan(combine, (a, u), axis=1)
    return h.astype(x.dtype)


def setup_inputs(seed: int = 0) -> dict:
    key = jax.random.key(seed)
    ks = jax.random.split(key, 20)
    f32 = jnp.float32
    L = DEPTH
    w_in = jax.random.normal(ks[1], (L, D_MODEL, IN_WIDTH), f32) * D_MODEL ** -0.5
    w_in = w_in.at[:, :, 2 * ATTN_WIDTH:3 * ATTN_WIDTH].multiply(DEEPNORM_BETA)
    conv_w = jax.random.normal(ks[2], (L, CONV_WIDTH, LRU_WIDTH), f32) * CONV_WIDTH ** -0.5
    conv_b = 0.02 * jax.random.normal(ks[3], (L, LRU_WIDTH), f32)
    w_rg = jax.random.normal(ks[4], (L, N_LRU_BLOCKS, LRU_BLOCK, LRU_BLOCK), f32) * LRU_BLOCK ** -0.5
    b_rg = 0.02 * jax.random.normal(ks[5], (L, LRU_WIDTH), f32)
    w_ig = jax.random.normal(ks[6], (L, N_LRU_BLOCKS, LRU_BLOCK, LRU_BLOCK), f32) * LRU_BLOCK ** -0.5
    b_ig = 0.02 * jax.random.normal(ks[7], (L, LRU_WIDTH), f32)
    a_c = jax.random.uniform(ks[8], (L, LRU_WIDTH), f32, 0.9, 0.999)
    a0 = a_c ** (1.0 / LRU_C)
    lru_lambda = jnp.log(a0) - jnp.log1p(-a0)
    w_out = jax.random.normal(ks[9], (L, D_MODEL, D_MODEL), f32) * D_MODEL ** -0.5 * DEEPNORM_BETA
    ln1_g = 1.0 + 0.02 * jax.random.normal(ks[10], (L, D_MODEL), f32)
    ln1_b = 0.02 * jax.random.normal(ks[11], (L, D_MODEL), f32)
    w_up = jax.random.normal(ks[12], (L, D_MODEL, D_FF), f32) * D_MODEL ** -0.5 * DEEPNORM_BETA
    b_up = 0.02 * jax.random.normal(ks[13], (L, D_FF), f32)
    w_down = jax.random.normal(ks[14], (L, D_FF, D_MODEL), f32) * D_FF ** -0.5 * DEEPNORM_BETA
    b_down = 0.02 * jax.random.normal(ks[15], (L, D_MODEL), f32)
    ln2_g = 1.0 + 0.02 * jax.random.normal(ks[16], (L, D_MODEL), f32)
    ln2_b = 0.02 * jax.random.normal(ks[17], (L, D_MODEL), f32)
    x = jax.random.normal(ks[0], (BATCH, SEQ, D_MODEL), f32)
    return {"x": x, "w_in": w_in, "conv_w": conv_w, "conv_b": conv_b,
            "w_rg": w_rg, "b_rg": b_rg, "w_ig": w_ig, "b_ig": b_ig,
            "lru_lambda": lru_lambda, "w_out": w_out, "ln1_g": ln1_g, "ln1_b": ln1_b,
            "w_up": w_up, "b_up": b_up, "w_down": w_down, "b_down": b_down,
            "ln2_g": ln2_g, "ln2_b": ln2_b}


def reference(x, w_in, conv_w, conv_b, w_rg, b_rg, w_ig, b_ig, lru_lambda, w_out,
              ln1_g, ln1_b, w_up, b_up, w_down, b_down, ln2_g, ln2_b):
    B, S, _ = x.shape
    for l in range(DEPTH):
        proj = jnp.einsum('bsd,de->bse', x, w_in[l])
        q, k, v, xr, gr = jnp.split(
            proj, [ATTN_WIDTH, 2 * ATTN_WIDTH, 3 * ATTN_WIDTH, 3 * ATTN_WIDTH + LRU_WIDTH], axis=-1)
        hs = (B, S, N_ATTN_HEADS, HEAD_DIM)
        attn = moba_attention(q.reshape(hs), k.reshape(hs), v.reshape(hs))
        xr = causal_depthwise_conv(xr, conv_w[l], conv_b[l])
        lru = rg_lru(xr, w_rg[l], b_rg[l], w_ig[l], b_ig[l], lru_lambda[l]) * jax.nn.gelu(gr)
        mix = jnp.einsum('bse,ed->bsd', jnp.concatenate([attn, lru], axis=-1), w_out[l])
        x = layer_norm(DEEPNORM_ALPHA * x + mix, ln1_g[l], ln1_b[l])
        h = jnp.square(jax.nn.relu(jnp.einsum('bsd,df->bsf', x, w_up[l]) + b_up[l]))
        h = jnp.einsum('bsf,fd->bsd', h, w_down[l]) + b_down[l]
        x = layer_norm(DEEPNORM_ALPHA * x + h, ln2_g[l], ln2_b[l])
    return x
```

```python
import functools

import jax
import jax.numpy as jnp
from jax import lax
from jax.experimental import pallas as pl
from jax.experimental.pallas import tpu as pltpu

D_MODEL = 1024
ATTN_WIDTH = 512
N_HEADS = 8
HEAD_DIM = 64
LRU_WIDTH = 512
N_LRU_BLOCKS = 8
LRU_BLOCK = 64
CONV_WIDTH = 4
LRU_C = 8.0
MOBA_BLOCK = 256
MOBA_TOPK = 3
D_FF = 4096
LN_EPS = 1e-5
DEEPNORM_ALPHA = 2.0 ** 0.25
NEG_INF = -1e30

SUBLANES = 8
LANES = 128
HEADS_PER_LANE_TILE = LANES // HEAD_DIM
N_HEAD_PAIRS = N_HEADS // HEADS_PER_LANE_TILE

QKV_ROWS = 512
LRU_ROWS = 256
MLP_ROWS = 1024
MLP_FF_TILE = 512
VMEM_LIMIT_BYTES = 56 * 1024 * 1024

_F32 = jnp.float32
_BF16 = jnp.bfloat16
_NT_DIMS = (((1,), (1,)), ((), ()))


def _qkv_kernel(x_ref, w_ref, q_ref, k_ref, vt_ref, km_ref):
    xb = x_ref[0].astype(_BF16)
    p = jnp.dot(xb, w_ref[...], preferred_element_type=_F32)
    q_ref[0] = (p[:, :ATTN_WIDTH] * (HEAD_DIM ** -0.5)).astype(_BF16)
    kf = p[:, ATTN_WIDTH:2 * ATTN_WIDTH]
    k_ref[0] = kf.astype(_BF16)
    v = p[:, 2 * ATTN_WIDTH:]
    for s in range(QKV_ROWS // MOBA_BLOCK):
        rows = slice(s * MOBA_BLOCK, (s + 1) * MOBA_BLOCK)
        vt_ref[0, s] = v[rows].T.astype(_BF16)
        km_ref[0, s] = jnp.mean(kf[rows], axis=0, keepdims=True)


def _qkv_proj(x, w_qkv):
    B, S, D = x.shape
    nb = S // MOBA_BLOCK
    bps = QKV_ROWS // MOBA_BLOCK
    return pl.pallas_call(
        _qkv_kernel,
        grid=(B, S // QKV_ROWS),
        in_specs=[
            pl.BlockSpec((1, QKV_ROWS, D), lambda b, t: (b, t, 0)),
            pl.BlockSpec((D, 3 * ATTN_WIDTH), lambda b, t: (0, 0)),
        ],
        out_specs=[
            pl.BlockSpec((1, QKV_ROWS, ATTN_WIDTH), lambda b, t: (b, t, 0)),
            pl.BlockSpec((1, QKV_ROWS, ATTN_WIDTH), lambda b, t: (b, t, 0)),
            pl.BlockSpec((1, bps, ATTN_WIDTH, MOBA_BLOCK), lambda b, t: (b, t, 0, 0)),
            pl.BlockSpec((1, bps, 1, ATTN_WIDTH), lambda b, t: (b, t, 0, 0)),
        ],
        out_shape=[
            jax.ShapeDtypeStruct((B, S, ATTN_WIDTH), _BF16),
            jax.ShapeDtypeStruct((B, S, ATTN_WIDTH), _BF16),
            jax.ShapeDtypeStruct((B, nb, ATTN_WIDTH, MOBA_BLOCK), _BF16),
            jax.ShapeDtypeStruct((B, nb, 1, ATTN_WIDTH), _F32),
        ],
        compiler_params=pltpu.CompilerParams(
            dimension_semantics=("parallel", "parallel"),
            vmem_limit_bytes=VMEM_LIMIT_BYTES),
        name="qkv_proj",
    )(x, w_qkv)


def _attn_kernel(slopes_ref, q_ref, k_ref, vt_ref, km_ref, o_ref, selb_ref):
    hp = pl.program_id(1)
    nb = k_ref.shape[1] // MOBA_BLOCK
    blk = MOBA_BLOCK

    lane = lax.broadcasted_iota(jnp.int32, (blk, LANES), 1)
    key_row = lax.broadcasted_iota(jnp.int32, (blk, blk), 0)
    qry_col = lax.broadcasted_iota(jnp.int32, (blk, blk), 1)
    causal = key_row <= qry_col
    key_row_f = key_row.astype(_F32)
    blk_idx = lax.broadcasted_iota(jnp.int32, (nb, blk), 0)
    km_lane = lax.broadcasted_iota(jnp.int32, (nb, LANES), 1)

    def q_block(j, _):
        q_rows = pl.ds(pl.multiple_of(j * blk, blk), blk)
        q2 = q_ref[0, q_rows, :]
        out_t = []
        for hh in range(HEADS_PER_LANE_TILE):
            lo = hh * HEAD_DIM
            slope = slopes_ref[hp * HEADS_PER_LANE_TILE + hh]
            qm = jnp.where((lane >= lo) & (lane < lo + HEAD_DIM), q2, jnp.zeros_like(q2))
            kb0 = key_row_f * slope

            km = km_ref[0, :, 0, :]
            kmm = jnp.where((km_lane >= lo) & (km_lane < lo + HEAD_DIM), km, 0.0).astype(_BF16)
            gate = lax.dot_general(kmm, qm, _NT_DIMS, preferred_element_type=_F32)
            rank = jnp.zeros((nb, blk), jnp.int32)
            for n2 in range(nb):
                g2 = gate[n2:n2 + 1, :]
                beats = (g2 > gate) | ((g2 == gate) & (n2 < blk_idx))
                rank = rank + jnp.where(beats & (n2 < j), 1, 0)
            sel = (blk_idx < j) & (rank < MOBA_TOPK)
            selb_ref[hh] = jnp.where(sel, 0.0, NEG_INF)

            k_own = k_ref[0, q_rows, :]
            z = lax.dot_general(k_own, qm, _NT_DIMS, preferred_element_type=_F32) + kb0
            z = jnp.where(causal, z, NEG_INF)
            m0 = jnp.max(z, axis=0, keepdims=True)
            p = jnp.exp(z - m0)
            l0 = jnp.sum(p, axis=0, keepdims=True)
            vt = vt_ref[0, j, lo:lo + HEAD_DIM, :]
            acc0 = jnp.dot(vt, p.astype(_BF16), preferred_element_type=_F32)

            def past_block(i, carry):
                m, l, acc = carry
                k_rows = pl.ds(pl.multiple_of(i * blk, blk), blk)
                zi = lax.dot_general(k_ref[0, k_rows, :], qm, _NT_DIMS,
                                     preferred_element_type=_F32) + kb0
                c = slope * ((i - j) * blk).astype(_F32)
                sb = selb_ref[hh, pl.ds(i, 1), :]
                bm = jnp.max(zi, axis=0, keepdims=True) + c + sb
                m_new = jnp.maximum(m, bm)
                alpha = jnp.exp(m - m_new)
                pi = jnp.exp(zi - (m_new - c - sb))
                l_new = alpha * l + jnp.sum(pi, axis=0, keepdims=True)
                vti = vt_ref[0, i, lo:lo + HEAD_DIM, :]
                acc_new = alpha * acc + jnp.dot(vti, pi.astype(_BF16), preferred_element_type=_F32)
                return m_new, l_new, acc_new

            _, l, acc = lax.fori_loop(0, j, past_block, (m0, l0, acc0))
            out_t.append(acc / l)
        o_ref[0, q_rows, :] = jnp.concatenate(out_t, axis=0).T.astype(o_ref.dtype)
        return 0

    lax.fori_loop(0, nb, q_block, 0)


def _moba_attention(slopes, q, k, vt, km):
    B, S, _ = q.shape
    nb = S // MOBA_BLOCK
    return pl.pallas_call(
        _attn_kernel,
        grid=(B, N_HEAD_PAIRS),
        in_specs=[
            pl.BlockSpec(memory_space=pltpu.SMEM),
            pl.BlockSpec((1, S, LANES), lambda b, h: (b, 0, h)),
            pl.BlockSpec((1, S, LANES), lambda b, h: (b, 0, h)),
            pl.BlockSpec((1, nb, LANES, MOBA_BLOCK), lambda b, h: (b, 0, h, 0)),
            pl.BlockSpec((1, nb, 1, LANES), lambda b, h: (b, 0, 0, h)),
        ],
        out_specs=pl.BlockSpec((1, S, LANES), lambda b, h: (b, 0, h)),
        out_shape=jax.ShapeDtypeStruct((B, S, ATTN_WIDTH), _BF16),
        scratch_shapes=[pltpu.VMEM((HEADS_PER_LANE_TILE, nb, MOBA_BLOCK), _F32)],
        compiler_params=pltpu.CompilerParams(
            dimension_semantics=("parallel", "parallel"),
            vmem_limit_bytes=VMEM_LIMIT_BYTES),
        name="moba_attention",
    )(slopes, q, k, vt, km)


def _lru_kernel(x_ref, w_ref, cw_ref, cb_ref, wrg_ref, brg_ref, wig_ref, big_ref, lam_ref,
                o_ref, xbuf, a_s, u_s, hcar):
    t = pl.program_id(1)
    R = LRU_ROWS
    C = LRU_WIDTH
    PAD = SUBLANES

    @pl.when(t == 0)
    def _():
        xbuf[0:PAD, :] = jnp.zeros((PAD, C), _F32)
        hcar[...] = jnp.zeros_like(hcar)

    p = jnp.dot(x_ref[0].astype(_BF16), w_ref[...], preferred_element_type=_F32)
    xr = p[:, :C]
    gr = p[:, C:]

    xbuf[PAD:PAD + R, :] = xr
    y = cb_ref[...] + cw_ref[CONV_WIDTH - 1:CONV_WIDTH, :] * xr
    for d in range(1, CONV_WIDTH):
        y = y + cw_ref[CONV_WIDTH - 1 - d:CONV_WIDTH - d, :] * xbuf[PAD - d:PAD - d + R, :]
    xbuf[0:PAD, :] = xr[R - PAD:R, :]

    yb = y.astype(_BF16)
    half = C // 2

    def gate_lin(w3_ref, b_ref):
        parts = [jnp.dot(yb[:, s * half:(s + 1) * half], w3_ref[s], preferred_element_type=_F32)
                 for s in range(2)]
        return jnp.concatenate(parts, axis=1) + b_ref[...]

    r = jax.nn.sigmoid(gate_lin(wrg_ref, brg_ref))
    ig = jax.nn.sigmoid(gate_lin(wig_ref, big_ref))
    lam = lam_ref[...]
    softplus_neg_lam = jnp.maximum(-lam, 0.0) + jnp.log(1.0 + jnp.exp(-jnp.abs(lam)))
    log_a = (-LRU_C) * r * softplus_neg_lam
    a = jnp.exp(log_a)
    u = jnp.sqrt(1.0 - jnp.exp(2.0 * log_a)) * (ig * y)
    a_s[...] = a
    u_s[...] = u

    row = lax.broadcasted_iota(jnp.int32, (SUBLANES, C), 0)

    def group(g, carry):
        rows = pl.ds(pl.multiple_of(g * SUBLANES, SUBLANES), SUBLANES)
        ag = a_s[rows, :]
        ug = u_s[rows, :]
        for d in (1, 2, 4):
            a_sh = pltpu.roll(ag, d, 0)
            u_sh = pltpu.roll(ug, d, 0)
            m = row >= d
            ug = jnp.where(m, ag * u_sh + ug, ug)
            ag = jnp.where(m, ag * a_sh, ag)
        h = ag * carry + ug
        u_s[rows, :] = h
        return jnp.broadcast_to(h[SUBLANES - 1:SUBLANES, :], (SUBLANES, C))

    hcar[...] = lax.fori_loop(0, R // SUBLANES, group, hcar[...])

    gelu = 0.5 * gr * (1.0 + jnp.tanh(0.7978845608028654 * (gr + 0.044715 * (gr * gr * gr))))
    o_ref[0] = (u_s[...] * gelu).astype(o_ref.dtype)


def _lru_branch(x, w_lru, conv_w, conv_b, wrg, b_rg, wig, b_ig, lam):
    B, S, D = x.shape
    C = LRU_WIDTH
    full = lambda shape: pl.BlockSpec(shape, lambda b, t: (0,) * len(shape))
    return pl.pallas_call(
        _lru_kernel,
        grid=(B, S // LRU_ROWS),
        in_specs=[
            pl.BlockSpec((1, LRU_ROWS, D), lambda b, t: (b, t, 0)),
            full((D, 2 * C)),
            full((CONV_WIDTH, C)),
            full((1, C)),
            full((2, C // 2, C // 2)),
            full((1, C)),
            full((2, C // 2, C // 2)),
            full((1, C)),
            full((1, C)),
        ],
        out_specs=pl.BlockSpec((1, LRU_ROWS, C), lambda b, t: (b, t, 0)),
        out_shape=jax.ShapeDtypeStruct((B, S, C), _BF16),
        scratch_shapes=[
            pltpu.VMEM((LRU_ROWS + SUBLANES, C), _F32),
            pltpu.VMEM((LRU_ROWS, C), _F32),
            pltpu.VMEM((LRU_ROWS, C), _F32),
            pltpu.VMEM((SUBLANES, C), _F32),
        ],
        compiler_params=pltpu.CompilerParams(
            dimension_semantics=("parallel", "arbitrary"),
            vmem_limit_bytes=VMEM_LIMIT_BYTES),
        name="lru_branch",
    )(x, w_lru, conv_w, conv_b, wrg, b_rg, wig, b_ig, lam)


def _layer_norm(y, g, b):
    mu = jnp.mean(y, axis=-1, keepdims=True)
    yc = y - mu
    var = jnp.mean(yc * yc, axis=-1, keepdims=True)
    return yc * lax.rsqrt(var + LN_EPS) * g + b


def _mlp_kernel(x_ref, attn_ref, lru_ref, wo_ref, g1_ref, b1_ref, wup_ref, bup_ref,
                wdn_ref, bdn_ref, g2_ref, b2_ref, o_ref, x1_s, x1b_s, acc_s):
    f = pl.program_id(1)

    @pl.when(f == 0)
    def _():
        mix = jnp.dot(attn_ref[...], wo_ref[0:ATTN_WIDTH, :], preferred_element_type=_F32)
        mix = mix + jnp.dot(lru_ref[...], wo_ref[ATTN_WIDTH:, :], preferred_element_type=_F32)
        x1 = _layer_norm(DEEPNORM_ALPHA * x_ref[...] + mix, g1_ref[...], b1_ref[...])
        x1_s[...] = x1
        x1b_s[...] = x1.astype(_BF16)
        acc_s[...] = jnp.zeros_like(acc_s)

    h = jnp.dot(x1b_s[...], wup_ref[...], preferred_element_type=_F32) + bup_ref[...]
    h = jnp.maximum(h, 0.0)
    h = (h * h).astype(_BF16)
    acc_s[...] += jnp.dot(h, wdn_ref[...], preferred_element_type=_F32)

    @pl.when(f == pl.num_programs(1) - 1)
    def _():
        y = DEEPNORM_ALPHA * x1_s[...] + acc_s[...] + bdn_ref[...]
        o_ref[...] = _layer_norm(y, g2_ref[...], b2_ref[...])


def _outproj_mlp(x2, attn2, lru2, w_out, g1, b1, w_up, b_up, w_down, b_down, g2, b2):
    M, D = x2.shape
    F = w_up.shape[1]
    tm, tf = MLP_ROWS, MLP_FF_TILE
    row_vec = lambda n: pl.BlockSpec((1, n), lambda i, f: (0, 0))
    return pl.pallas_call(
        _mlp_kernel,
        grid=(M // tm, F // tf),
        in_specs=[
            pl.BlockSpec((tm, D), lambda i, f: (i, 0)),
            pl.BlockSpec((tm, ATTN_WIDTH), lambda i, f: (i, 0)),
            pl.BlockSpec((tm, LRU_WIDTH), lambda i, f: (i, 0)),
            pl.BlockSpec((D, D), lambda i, f: (0, 0)),
            row_vec(D), row_vec(D),
            pl.BlockSpec((D, tf), lambda i, f: (0, f)),
            pl.BlockSpec((1, tf), lambda i, f: (0, f)),
            pl.BlockSpec((tf, D), lambda i, f: (f, 0)),
            row_vec(D), row_vec(D), row_vec(D),
        ],
        out_specs=pl.BlockSpec((tm, D), lambda i, f: (i, 0)),
        out_shape=jax.ShapeDtypeStruct((M, D), _F32),
        scratch_shapes=[
            pltpu.VMEM((tm, D), _F32),
            pltpu.VMEM((tm, D), _BF16),
            pltpu.VMEM((tm, D), _F32),
        ],
        compiler_params=pltpu.CompilerParams(
            dimension_semantics=("parallel", "arbitrary"),
            vmem_limit_bytes=VMEM_LIMIT_BYTES),
        name="outproj_mlp",
    )(x2, attn2, lru2, w_out, g1, b1, w_up, b_up, w_down, b_down, g2, b2)


def _block_diag_halves(w):
    n_half = N_LRU_BLOCKS // 2
    eye = jnp.eye(n_half, dtype=w.dtype)
    wh = w.reshape(2, n_half, LRU_BLOCK, LRU_BLOCK)
    full = jnp.einsum('snde,nm->sndme', wh, eye)
    return full.reshape(2, n_half * LRU_BLOCK, n_half * LRU_BLOCK)


def kernel(x, w_in, conv_w, conv_b, w_rg, b_rg, w_ig, b_ig, lru_lambda, w_out, ln1_g, ln1_b,
           w_up, b_up, w_down, b_down, ln2_g, ln2_b):
    B, S, D = x.shape
    depth = w_in.shape[0]
    row = lambda v: v.reshape(1, -1)
    h = jnp.arange(1, N_HEADS + 1, dtype=_F32)
    slopes = jnp.exp2(-8.0 * h / N_HEADS)
    for l in range(depth):
        w_qkv = w_in[l, :, :3 * ATTN_WIDTH].astype(_BF16)
        w_lru = w_in[l, :, 3 * ATTN_WIDTH:].astype(_BF16)
        q, k, vt, km = _qkv_proj(x, w_qkv)
        attn = _moba_attention(slopes, q, k, vt, km)
        lru = _lru_branch(x, w_lru, conv_w[l], row(conv_b[l]),
                          _block_diag_halves(w_rg[l]).astype(_BF16), row(b_rg[l]),
                          _block_diag_halves(w_ig[l]).astype(_BF16), row(b_ig[l]),
                          row(lru_lambda[l]))
        out = _outproj_mlp(
            x.reshape(B * S, D), attn.reshape(B * S, ATTN_WIDTH), lru.reshape(B * S, LRU_WIDTH),
            w_out[l].astype(_BF16), row(ln1_g[l]), row(ln1_b[l]),
            w_up[l].astype(_BF16), row(b_up[l]), w_down[l].astype(_BF16), row(b_down[l]),
            row(ln2_g[l]), row(ln2_b[l]))
        x = out.reshape(B, S, D)
    return x
```

```python
import jax
import jax.numpy as jnp
from jax import lax
from jax.experimental import pallas as pl
from jax.experimental.pallas import tpu as pltpu

D_MODEL = 1024
ATTN_WIDTH = 512
N_HEADS = 8
HEAD_DIM = 64
LRU_WIDTH = 512
N_LRU_BLOCKS = 8
LRU_BLOCK = 64
CONV_WIDTH = 4
LRU_C = 8.0
MOBA_BLOCK = 256
MOBA_TOPK = 3
D_FF = 4096
LN_EPS = 1e-5
DEEPNORM_ALPHA = 2.0 ** 0.25
NEG_INF = -1e30

SUBLANES = 8
LANES = 128
HEADS_PER_LANE_TILE = LANES // HEAD_DIM
N_HEAD_PAIRS = N_HEADS // HEADS_PER_LANE_TILE
AUG_WIDTH = N_HEADS * LANES
SEL_LANE0 = HEAD_DIM
ALIBI_PIECES = 3

QKV_ROWS = 512
LRU_ROWS = 256
MLP_ROWS = 1024
MLP_FF_TILE = 512
VMEM_LIMIT_BYTES = 56 * 1024 * 1024

_F32 = jnp.float32
_BF16 = jnp.bfloat16
_NT_DIMS = (((1,), (1,)), ((), ()))


def _qkv_kernel(x_ref, w_ref, qx_ref, kx_ref, q_ref, k_ref, vt_ref, km_ref):
    R = QKV_ROWS
    xb = x_ref[0].astype(_BF16)
    p = jnp.dot(xb, w_ref[...], preferred_element_type=_F32)
    data_lane = lax.broadcasted_iota(jnp.int32, (R, LANES), 1) < HEAD_DIM
    qx = qx_ref[...]
    for pair in range(N_HEAD_PAIRS):
        q2 = p[:, pair * LANES:(pair + 1) * LANES] * (HEAD_DIM ** -0.5)
        k2 = p[:, ATTN_WIDTH + pair * LANES:ATTN_WIDTH + (pair + 1) * LANES]
        for hh in range(HEADS_PER_LANE_TILE):
            cols = slice((pair * HEADS_PER_LANE_TILE + hh) * LANES,
                         (pair * HEADS_PER_LANE_TILE + hh + 1) * LANES)
            qh = q2 if hh == 0 else pltpu.roll(q2, HEAD_DIM, 1)
            kh = k2 if hh == 0 else pltpu.roll(k2, HEAD_DIM, 1)
            q_ref[0, :, cols] = jnp.where(data_lane, qh, qx).astype(_BF16)
            k_ref[0, :, cols] = jnp.where(data_lane, kh.astype(_BF16), kx_ref[:, cols])
            kpad = jnp.where(data_lane, kh, 0.0)
            for s in range(R // MOBA_BLOCK):
                km_ref[0, s, :, cols] = jnp.mean(kpad[s * MOBA_BLOCK:(s + 1) * MOBA_BLOCK],
                                                 axis=0, keepdims=True)
    vt_ref[0] = p[:, 2 * ATTN_WIDTH:].T.astype(_BF16)


def _qkv_proj(x, w_qkv, qx, kx):
    B, S, D = x.shape
    nb = S // MOBA_BLOCK
    bps = QKV_ROWS // MOBA_BLOCK
    return pl.pallas_call(
        _qkv_kernel,
        grid=(B, S // QKV_ROWS),
        in_specs=[
            pl.BlockSpec((1, QKV_ROWS, D), lambda b, t: (b, t, 0)),
            pl.BlockSpec((D, 3 * ATTN_WIDTH), lambda b, t: (0, 0)),
            pl.BlockSpec((1, LANES), lambda b, t: (0, 0)),
            pl.BlockSpec((QKV_ROWS, AUG_WIDTH), lambda b, t: (t, 0)),
        ],
        out_specs=[
            pl.BlockSpec((1, QKV_ROWS, AUG_WIDTH), lambda b, t: (b, t, 0)),
            pl.BlockSpec((1, QKV_ROWS, AUG_WIDTH), lambda b, t: (b, t, 0)),
            pl.BlockSpec((1, ATTN_WIDTH, QKV_ROWS), lambda b, t: (b, 0, t)),
            pl.BlockSpec((1, bps, 1, AUG_WIDTH), lambda b, t: (b, t, 0, 0)),
        ],
        out_shape=[
            jax.ShapeDtypeStruct((B, S, AUG_WIDTH), _BF16),
            jax.ShapeDtypeStruct((B, S, AUG_WIDTH), _BF16),
            jax.ShapeDtypeStruct((B, ATTN_WIDTH, S), _BF16),
            jax.ShapeDtypeStruct((B, nb, 1, AUG_WIDTH), _F32),
        ],
        compiler_params=pltpu.CompilerParams(
            dimension_semantics=("parallel", "parallel"),
            vmem_limit_bytes=VMEM_LIMIT_BYTES),
        name="qkv_proj",
    )(x, w_qkv, qx, kx)


def _aug_constants(seq_len, slopes):
    nb = seq_len // MOBA_BLOCK
    alibi_lane0 = SEL_LANE0 + nb
    lane = jnp.arange(LANES)
    qx = ((lane >= alibi_lane0) & (lane < alibi_lane0 + ALIBI_PIECES)).astype(_F32).reshape(1, LANES)

    pos = jnp.arange(seq_len)
    onehot = (pos[:, None] // MOBA_BLOCK == jnp.arange(nb)[None, :]).astype(_F32)
    val = slopes[None, :] * (pos[:, None] - seq_len // 2).astype(_F32)

    def trunc_bf16(v):
        return lax.bitcast_convert_type(
            lax.bitcast_convert_type(v, jnp.uint32) & jnp.uint32(0xFFFF0000), _F32)

    hi = trunc_bf16(val)
    mid = trunc_bf16(val - hi)
    lo = val - hi - mid
    pieces = jnp.stack([hi, mid, lo], axis=-1)
    extras = jnp.concatenate([
        jnp.zeros((seq_len, N_HEADS, HEAD_DIM), _F32),
        jnp.broadcast_to(onehot[:, None, :], (seq_len, N_HEADS, nb)),
        pieces,
        jnp.zeros((seq_len, N_HEADS, LANES - alibi_lane0 - ALIBI_PIECES), _F32)], axis=-1)
    return qx, extras.reshape(seq_len, AUG_WIDTH).astype(_BF16)


def _attn_kernel(q_ref, k_ref, vt_ref, km_ref, o_ref):
    blk = MOBA_BLOCK
    nb = k_ref.shape[1] // blk

    lane = lax.broadcasted_iota(jnp.int32, (blk, LANES), 1)
    sel_lane = (lane >= SEL_LANE0) & (lane < SEL_LANE0 + nb)
    key_row = lax.broadcasted_iota(jnp.int32, (blk, blk), 0)
    qry_col = lax.broadcasted_iota(jnp.int32, (blk, blk), 1)
    causal = key_row <= qry_col
    blk_idx = lax.broadcasted_iota(jnp.int32, (nb, blk), 0)

    for j in range(nb):
        nk = (j + 1) * blk
        q_rows = slice(j * blk, nk)
        out_t = []
        for hh in range(HEADS_PER_LANE_TILE):
            cols = slice(hh * LANES, (hh + 1) * LANES)
            qa = q_ref[0, q_rows, cols]
            if j > MOBA_TOPK:
                kmh = km_ref[0, :, 0, cols].astype(_BF16)
                gate = lax.dot_general(kmh, qa, _NT_DIMS, preferred_element_type=_F32)
                rank = jnp.zeros((nb, blk), jnp.int32)
                for n2 in range(j):
                    g2 = gate[n2:n2 + 1, :]
                    beats = (g2 > gate) | ((g2 == gate) & (n2 < blk_idx))
                    rank = rank + beats.astype(jnp.int32)
                keep = ((blk_idx < j) & (rank < MOBA_TOPK)) | (blk_idx >= j)
                selb = jnp.where(keep, 0.0, NEG_INF)
                selb_rows = jnp.concatenate(
                    [jnp.zeros((SEL_LANE0, blk), _F32), selb,
                     jnp.zeros((LANES - SEL_LANE0 - nb, blk), _F32)], axis=0)
                qa = jnp.where(sel_lane, selb_rows.T.astype(_BF16), qa)

            z = lax.dot_general(k_ref[0, 0:nk, cols], qa, _NT_DIMS,
                                preferred_element_type=_F32)
            z_own = jnp.where(causal, z[nk - blk:, :], NEG_INF)
            z = z_own if j == 0 else jnp.concatenate([z[:nk - blk, :], z_own], axis=0)
            m = jnp.max(z, axis=0, keepdims=True)
            p = jnp.exp(z - m)
            l = jnp.sum(p, axis=0, keepdims=True)
            vt = vt_ref[0, hh * HEAD_DIM:(hh + 1) * HEAD_DIM, 0:nk]
            acc = jnp.dot(vt, p.astype(_BF16), preferred_element_type=_F32)
            out_t.append(acc / l)
        o_ref[0, q_rows, :] = jnp.concatenate(out_t, axis=0).T.astype(o_ref.dtype)


def _moba_attention(q, k, vt, km):
    B, S, _ = q.shape
    nb = S // MOBA_BLOCK
    pair_w = HEADS_PER_LANE_TILE * LANES
    return pl.pallas_call(
        _attn_kernel,
        grid=(B, N_HEAD_PAIRS),
        in_specs=[
            pl.BlockSpec((1, S, pair_w), lambda b, h: (b, 0, h)),
            pl.BlockSpec((1, S, pair_w), lambda b, h: (b, 0, h)),
            pl.BlockSpec((1, LANES, S), lambda b, h: (b, h, 0)),
            pl.BlockSpec((1, nb, 1, pair_w), lambda b, h: (b, 0, 0, h)),
        ],
        out_specs=pl.BlockSpec((1, S, LANES), lambda b, h: (b, 0, h)),
        out_shape=jax.ShapeDtypeStruct((B, S, ATTN_WIDTH), _BF16),
        compiler_params=pltpu.CompilerParams(
            dimension_semantics=("parallel", "parallel"),
            vmem_limit_bytes=VMEM_LIMIT_BYTES),
        name="moba_attention",
    )(q, k, vt, km)


def _lru_kernel(x_ref, w_ref, cw_ref, cb_ref, wrg_ref, brg_ref, wig_ref, big_ref, lam_ref,
                o_ref, xbuf, a_s, u_s, hcar):
    t = pl.program_id(1)
    R = LRU_ROWS
    C = LRU_WIDTH
    PAD = SUBLANES

    @pl.when(t == 0)
    def _():
        xbuf[0:PAD, :] = jnp.zeros((PAD, C), _F32)
        hcar[...] = jnp.zeros_like(hcar)

    p = jnp.dot(x_ref[0].astype(_BF16), w_ref[...], preferred_element_type=_F32)
    xr = p[:, :C]
    gr = p[:, C:]

    xbuf[PAD:PAD + R, :] = xr
    y = cb_ref[...] + cw_ref[CONV_WIDTH - 1:CONV_WIDTH, :] * xr
    for d in range(1, CONV_WIDTH):
        y = y + cw_ref[CONV_WIDTH - 1 - d:CONV_WIDTH - d, :] * xbuf[PAD - d:PAD - d + R, :]
    xbuf[0:PAD, :] = xr[R - PAD:R, :]

    yb = y.astype(_BF16)
    half = C // 2

    def gate_lin(w3_ref, b_ref):
        parts = [jnp.dot(yb[:, s * half:(s + 1) * half], w3_ref[s], preferred_element_type=_F32)
                 for s in range(2)]
        return jnp.concatenate(parts, axis=1) + b_ref[...]

    r = jax.nn.sigmoid(gate_lin(wrg_ref, brg_ref))
    ig = jax.nn.sigmoid(gate_lin(wig_ref, big_ref))
    lam = lam_ref[...]
    softplus_neg_lam = jnp.maximum(-lam, 0.0) + jnp.log(1.0 + jnp.exp(-jnp.abs(lam)))
    log_a = (-LRU_C) * r * softplus_neg_lam
    a = jnp.exp(log_a)
    u = jnp.sqrt(1.0 - jnp.exp(2.0 * log_a)) * (ig * y)
    a_s[...] = a
    u_s[...] = u

    row = lax.broadcasted_iota(jnp.int32, (SUBLANES, C), 0)

    def group(g, carry):
        rows = pl.ds(pl.multiple_of(g * SUBLANES, SUBLANES), SUBLANES)
        ag = a_s[rows, :]
        ug = u_s[rows, :]
        for d in (1, 2, 4):
            a_sh = pltpu.roll(ag, d, 0)
            u_sh = pltpu.roll(ug, d, 0)
            m = row >= d
            ug = jnp.where(m, ag * u_sh + ug, ug)
            ag = jnp.where(m, ag * a_sh, ag)
        h = ag * carry + ug
        u_s[rows, :] = h
        return jnp.broadcast_to(h[SUBLANES - 1:SUBLANES, :], (SUBLANES, C))

    hcar[...] = lax.fori_loop(0, R // SUBLANES, group, hcar[...])

    gelu = 0.5 * gr * (1.0 + jnp.tanh(0.7978845608028654 * (gr + 0.044715 * (gr * gr * gr))))
    o_ref[0] = (u_s[...] * gelu).astype(o_ref.dtype)


def _lru_branch(x, w_lru, conv_w, conv_b, wrg, b_rg, wig, b_ig, lam):
    B, S, D = x.shape
    C = LRU_WIDTH
    full = lambda shape: pl.BlockSpec(shape, lambda b, t: (0,) * len(shape))
    return pl.pallas_call(
        _lru_kernel,
        grid=(B, S // LRU_ROWS),
        in_specs=[
            pl.BlockSpec((1, LRU_ROWS, D), lambda b, t: (b, t, 0)),
            full((D, 2 * C)),
            full((CONV_WIDTH, C)),
            full((1, C)),
            full((2, C // 2, C // 2)),
            full((1, C)),
            full((2, C // 2, C // 2)),
            full((1, C)),
            full((1, C)),
        ],
        out_specs=pl.BlockSpec((1, LRU_ROWS, C), lambda b, t: (b, t, 0)),
        out_shape=jax.ShapeDtypeStruct((B, S, C), _BF16),
        scratch_shapes=[
            pltpu.VMEM((LRU_ROWS + SUBLANES, C), _F32),
            pltpu.VMEM((LRU_ROWS, C), _F32),
            pltpu.VMEM((LRU_ROWS, C), _F32),
            pltpu.VMEM((SUBLANES, C), _F32),
        ],
        compiler_params=pltpu.CompilerParams(
            dimension_semantics=("parallel", "arbitrary"),
            vmem_limit_bytes=VMEM_LIMIT_BYTES),
        name="lru_branch",
    )(x, w_lru, conv_w, conv_b, wrg, b_rg, wig, b_ig, lam)


def _layer_norm(y, g, b):
    mu = jnp.mean(y, axis=-1, keepdims=True)
    yc = y - mu
    var = jnp.mean(yc * yc, axis=-1, keepdims=True)
    return yc * lax.rsqrt(var + LN_EPS) * g + b


def _mlp_kernel(x_ref, attn_ref, lru_ref, wo_ref, g1_ref, b1_ref, wup_ref, bup_ref,
                wdn_ref, bdn_ref, g2_ref, b2_ref, o_ref, x1_s, x1b_s, acc_s):
    f = pl.program_id(1)

    @pl.when(f == 0)
    def _():
        mix = jnp.dot(attn_ref[...], wo_ref[0:ATTN_WIDTH, :], preferred_element_type=_F32)
        mix = mix + jnp.dot(lru_ref[...], wo_ref[ATTN_WIDTH:, :], preferred_element_type=_F32)
        x1 = _layer_norm(DEEPNORM_ALPHA * x_ref[...] + mix, g1_ref[...], b1_ref[...])
        x1_s[...] = x1
        x1b_s[...] = x1.astype(_BF16)
        acc_s[...] = jnp.zeros_like(acc_s)

    h = jnp.dot(x1b_s[...], wup_ref[...], preferred_element_type=_F32) + bup_ref[...]
    h = jnp.maximum(h, 0.0)
    h = (h * h).astype(_BF16)
    acc_s[...] += jnp.dot(h, wdn_ref[...], preferred_element_type=_F32)

    @pl.when(f == pl.num_programs(1) - 1)
    def _():
        y = DEEPNORM_ALPHA * x1_s[...] + acc_s[...] + bdn_ref[...]
        o_ref[...] = _layer_norm(y, g2_ref[...], b2_ref[...])


def _outproj_mlp(x2, attn2, lru2, w_out, g1, b1, w_up, b_up, w_down, b_down, g2, b2):
    M, D = x2.shape
    F = w_up.shape[1]
    tm, tf = MLP_ROWS, MLP_FF_TILE
    row_vec = lambda n: pl.BlockSpec((1, n), lambda i, f: (0, 0))
    return pl.pallas_call(
        _mlp_kernel,
        grid=(M // tm, F // tf),
        in_specs=[
            pl.BlockSpec((tm, D), lambda i, f: (i, 0)),
            pl.BlockSpec((tm, ATTN_WIDTH), lambda i, f: (i, 0)),
            pl.BlockSpec((tm, LRU_WIDTH), lambda i, f: (i, 0)),
            pl.BlockSpec((D, D), lambda i, f: (0, 0)),
            row_vec(D), row_vec(D),
            pl.BlockSpec((D, tf), lambda i, f: (0, f)),
            pl.BlockSpec((1, tf), lambda i, f: (0, f)),
            pl.BlockSpec((tf, D), lambda i, f: (f, 0)),
            row_vec(D), row_vec(D), row_vec(D),
        ],
        out_specs=pl.BlockSpec((tm, D), lambda i, f: (i, 0)),
        out_shape=jax.ShapeDtypeStruct((M, D), _F32),
        scratch_shapes=[
            pltpu.VMEM((tm, D), _F32),
            pltpu.VMEM((tm, D), _BF16),
            pltpu.VMEM((tm, D), _F32),
        ],
        compiler_params=pltpu.CompilerParams(
            dimension_semantics=("parallel", "arbitrary"),
            vmem_limit_bytes=VMEM_LIMIT_BYTES),
        name="outproj_mlp",
    )(x2, attn2, lru2, w_out, g1, b1, w_up, b_up, w_down, b_down, g2, b2)


def _block_diag_halves(w):
    n_half = N_LRU_BLOCKS // 2
    eye = jnp.eye(n_half, dtype=w.dtype)
    wh = w.reshape(2, n_half, LRU_BLOCK, LRU_BLOCK)
    full = jnp.einsum('snde,nm->sndme', wh, eye)
    return full.reshape(2, n_half * LRU_BLOCK, n_half * LRU_BLOCK)


def kernel(x, w_in, conv_w, conv_b, w_rg, b_rg, w_ig, b_ig, lru_lambda, w_out, ln1_g, ln1_b,
           w_up, b_up, w_down, b_down, ln2_g, ln2_b):
    B, S, D = x.shape
    depth = w_in.shape[0]
    row = lambda v: v.reshape(1, -1)
    h = jnp.arange(1, N_HEADS + 1, dtype=_F32)
    slopes = jnp.exp2(-8.0 * h / N_HEADS)
    qx, kx = _aug_constants(S, slopes)
    for l in range(depth):
        w_qkv = w_in[l, :, :3 * ATTN_WIDTH].astype(_BF16)
        w_lru = w_in[l, :, 3 * ATTN_WIDTH:].astype(_BF16)
        q, k, vt, km = _qkv_proj(x, w_qkv, qx, kx)
        attn = _moba_attention(q, k, vt, km)
        lru = _lru_branch(x, w_lru, conv_w[l], row(conv_b[l]),
                          _block_diag_halves(w_rg[l]).astype(_BF16), row(b_rg[l]),
                          _block_diag_halves(w_ig[l]).astype(_BF16), row(b_ig[l]),
                          row(lru_lambda[l]))
        out = _outproj_mlp(
            x.reshape(B * S, D), attn.reshape(B * S, ATTN_WIDTH), lru.reshape(B * S, LRU_WIDTH),
            w_out[l].astype(_BF16), row(ln1_g[l]), row(ln1_b[l]),
            w_up[l].astype(_BF16), row(b_up[l]), w_down[l].astype(_BF16), row(b_down[l]),
            row(ln2_g[l]), row(ln2_b[l]))
        x = out.reshape(B, S, D)
    return x
```

```python
import jax
import jax.numpy as jnp
from jax import lax
from jax.experimental import pallas as pl
from jax.experimental.pallas import tpu as pltpu

D_MODEL = 1024
ATTN_WIDTH = 512
N_HEADS = 8
HEAD_DIM = 64
LRU_WIDTH = 512
N_LRU_BLOCKS = 8
LRU_BLOCK = 64
CONV_WIDTH = 4
LRU_C = 8.0
MOBA_BLOCK = 256
MOBA_TOPK = 3
D_FF = 4096
LN_EPS = 1e-5
DEEPNORM_ALPHA = 2.0 ** 0.25
NEG_INF = -1e30

SUBLANES = 8
LANES = 128
HEADS_PER_LANE_TILE = LANES // HEAD_DIM
N_HEAD_PAIRS = N_HEADS // HEADS_PER_LANE_TILE
AUG_WIDTH = N_HEADS * LANES
SEL_LANE0 = HEAD_DIM
ALIBI_PIECES = 3
VT_ROWS = HEAD_DIM + 16
LOG2E = 1.4426950408889634

QKV_ROWS = 512
LRU_ROWS = 256
MLP_ROWS = 1024
MLP_FF_TILE = 512
VMEM_LIMIT_BYTES = 56 * 1024 * 1024

_F32 = jnp.float32
_BF16 = jnp.bfloat16
_NT_DIMS = (((1,), (1,)), ((), ()))


def _qkv_kernel(x_ref, w_ref, qx_ref, kx_ref, q_ref, k_ref, vt_ref, km_ref):
    R = QKV_ROWS
    xb = x_ref[0].astype(_BF16)
    p = jnp.dot(xb, w_ref[...], preferred_element_type=_F32)
    data_lane = lax.broadcasted_iota(jnp.int32, (R, LANES), 1) < HEAD_DIM
    qx = qx_ref[...]
    for pair in range(N_HEAD_PAIRS):
        q2 = p[:, pair * LANES:(pair + 1) * LANES] * (HEAD_DIM ** -0.5 * LOG2E)
        k2 = p[:, ATTN_WIDTH + pair * LANES:ATTN_WIDTH + (pair + 1) * LANES]
        for hh in range(HEADS_PER_LANE_TILE):
            cols = slice((pair * HEADS_PER_LANE_TILE + hh) * LANES,
                         (pair * HEADS_PER_LANE_TILE + hh + 1) * LANES)
            qh = q2 if hh == 0 else pltpu.roll(q2, HEAD_DIM, 1)
            kh = k2 if hh == 0 else pltpu.roll(k2, HEAD_DIM, 1)
            q_ref[0, :, cols] = jnp.where(data_lane, qh, qx).astype(_BF16)
            k_ref[0, :, cols] = jnp.where(data_lane, kh.astype(_BF16), kx_ref[:, cols])
            kpad = jnp.where(data_lane, kh, 0.0)
            for s in range(R // MOBA_BLOCK):
                km_ref[0, s, :, cols] = jnp.mean(kpad[s * MOBA_BLOCK:(s + 1) * MOBA_BLOCK],
                                                 axis=0, keepdims=True)
    vt = p[:, 2 * ATTN_WIDTH:].T.astype(_BF16)
    ones_rows = (lax.broadcasted_iota(jnp.int32, (VT_ROWS - HEAD_DIM, R), 0) == 0).astype(_BF16)
    for h in range(N_HEADS):
        vt_ref[0, h * VT_ROWS:h * VT_ROWS + HEAD_DIM, :] = vt[h * HEAD_DIM:(h + 1) * HEAD_DIM, :]
        vt_ref[0, h * VT_ROWS + HEAD_DIM:(h + 1) * VT_ROWS, :] = ones_rows


def _qkv_proj(x, w_qkv, qx, kx):
    B, S, D = x.shape
    nb = S // MOBA_BLOCK
    bps = QKV_ROWS // MOBA_BLOCK
    return pl.pallas_call(
        _qkv_kernel,
        grid=(B, S // QKV_ROWS),
        in_specs=[
            pl.BlockSpec((1, QKV_ROWS, D), lambda b, t: (b, t, 0)),
            pl.BlockSpec((D, 3 * ATTN_WIDTH), lambda b, t: (0, 0)),
            pl.BlockSpec((1, LANES), lambda b, t: (0, 0)),
            pl.BlockSpec((QKV_ROWS, AUG_WIDTH), lambda b, t: (t, 0)),
        ],
        out_specs=[
            pl.BlockSpec((1, QKV_ROWS, AUG_WIDTH), lambda b, t: (b, t, 0)),
            pl.BlockSpec((1, QKV_ROWS, AUG_WIDTH), lambda b, t: (b, t, 0)),
            pl.BlockSpec((1, N_HEADS * VT_ROWS, QKV_ROWS), lambda b, t: (b, 0, t)),
            pl.BlockSpec((1, bps, 1, AUG_WIDTH), lambda b, t: (b, t, 0, 0)),
        ],
        out_shape=[
            jax.ShapeDtypeStruct((B, S, AUG_WIDTH), _BF16),
            jax.ShapeDtypeStruct((B, S, AUG_WIDTH), _BF16),
            jax.ShapeDtypeStruct((B, N_HEADS * VT_ROWS, S), _BF16),
            jax.ShapeDtypeStruct((B, nb, 1, AUG_WIDTH), _F32),
        ],
        compiler_params=pltpu.CompilerParams(
            dimension_semantics=("parallel", "parallel"),
            vmem_limit_bytes=VMEM_LIMIT_BYTES),
        name="qkv_proj",
    )(x, w_qkv, qx, kx)


def _aug_constants(seq_len, slopes):
    nb = seq_len // MOBA_BLOCK
    alibi_lane0 = SEL_LANE0 + nb
    lane = jnp.arange(LANES)
    qx = ((lane >= alibi_lane0) & (lane < alibi_lane0 + ALIBI_PIECES)).astype(_F32).reshape(1, LANES)

    pos = jnp.arange(seq_len)
    onehot = (pos[:, None] // MOBA_BLOCK == jnp.arange(nb)[None, :]).astype(_F32)
    val = (LOG2E * slopes)[None, :] * (pos[:, None] - seq_len // 2).astype(_F32)

    def trunc_bf16(v):
        return lax.bitcast_convert_type(
            lax.bitcast_convert_type(v, jnp.uint32) & jnp.uint32(0xFFFF0000), _F32)

    hi = trunc_bf16(val)
    mid = trunc_bf16(val - hi)
    lo = val - hi - mid
    pieces = jnp.stack([hi, mid, lo], axis=-1)
    extras = jnp.concatenate([
        jnp.zeros((seq_len, N_HEADS, HEAD_DIM), _F32),
        jnp.broadcast_to(onehot[:, None, :], (seq_len, N_HEADS, nb)),
        pieces,
        jnp.zeros((seq_len, N_HEADS, LANES - alibi_lane0 - ALIBI_PIECES), _F32)], axis=-1)
    return qx, extras.reshape(seq_len, AUG_WIDTH).astype(_BF16)


def _attn_kernel(q_ref, k_ref, vt_ref, km_ref, o_ref):
    blk = MOBA_BLOCK
    nb = k_ref.shape[1] // blk

    lane = lax.broadcasted_iota(jnp.int32, (blk, LANES), 1)
    sel_lane = (lane >= SEL_LANE0) & (lane < SEL_LANE0 + nb)
    key_row = lax.broadcasted_iota(jnp.int32, (blk, blk), 0)
    qry_col = lax.broadcasted_iota(jnp.int32, (blk, blk), 1)
    causal = key_row <= qry_col
    blk_idx = lax.broadcasted_iota(jnp.int32, (nb, blk), 0)

    def scores(j, hh):
        nk = (j + 1) * blk
        cols = slice(hh * LANES, (hh + 1) * LANES)
        qa = q_ref[0, j * blk:nk, cols]
        if j > MOBA_TOPK:
            kmh = km_ref[0, :, 0, cols].astype(_BF16)
            gate = lax.dot_general(kmh, qa, _NT_DIMS, preferred_element_type=_F32)
            rank = jnp.zeros((nb, blk), jnp.int32)
            for n2 in range(j):
                g2 = gate[n2:n2 + 1, :]
                beats = (g2 > gate) | ((g2 == gate) & (n2 < blk_idx))
                rank = rank + beats.astype(jnp.int32)
            keep = ((blk_idx < j) & (rank < MOBA_TOPK)) | (blk_idx >= j)
            selb = jnp.where(keep, 0.0, NEG_INF)
            selb_rows = jnp.concatenate(
                [jnp.zeros((SEL_LANE0, blk), _F32), selb,
                 jnp.zeros((LANES - SEL_LANE0 - nb, blk), _F32)], axis=0)
            qa = jnp.where(sel_lane, selb_rows.T.astype(_BF16), qa)
        z = lax.dot_general(k_ref[0, 0:nk, cols], qa, _NT_DIMS,
                            preferred_element_type=_F32)
        z_own = jnp.where(causal, z[nk - blk:, :], NEG_INF)
        z = z_own if j == 0 else jnp.concatenate([z[:nk - blk, :], z_own], axis=0)
        m = jnp.max(z, axis=0, keepdims=True)
        return z, m

    def probs(zm):
        z, m = zm
        return jnp.exp2(z - m).astype(_BF16)

    def weighted_values(j, hh, p):
        nk = (j + 1) * blk
        vt = vt_ref[0, hh * VT_ROWS:(hh + 1) * VT_ROWS, 0:nk]
        acc = jnp.dot(vt, p, preferred_element_type=_F32)
        return acc[0:HEAD_DIM, :] / acc[HEAD_DIM:HEAD_DIM + 1, :]

    items = [(j, hh) for j in range(nb) for hh in range(HEADS_PER_LANE_TILE)]
    n_items = len(items)
    zm = {0: scores(*items[0])}
    pls = {}
    outs = {}
    for t in range(n_items + 1):
        if t + 1 < n_items:
            zm[t + 1] = scores(*items[t + 1])
        if t < n_items:
            pls[t] = probs(zm.pop(t))
        if t >= 1:
            j, hh = items[t - 1]
            outs[hh] = weighted_values(j, hh, pls.pop(t - 1))
            if hh == HEADS_PER_LANE_TILE - 1:
                o_ref[0, j * blk:(j + 1) * blk, :] = jnp.concatenate(
                    [outs[h] for h in range(HEADS_PER_LANE_TILE)], axis=0).T.astype(o_ref.dtype)


def _moba_attention(q, k, vt, km):
    B, S, _ = q.shape
    nb = S // MOBA_BLOCK
    pair_w = HEADS_PER_LANE_TILE * LANES
    return pl.pallas_call(
        _attn_kernel,
        grid=(B, N_HEAD_PAIRS),
        in_specs=[
            pl.BlockSpec((1, S, pair_w), lambda b, h: (b, 0, h)),
            pl.BlockSpec((1, S, pair_w), lambda b, h: (b, 0, h)),
            pl.BlockSpec((1, HEADS_PER_LANE_TILE * VT_ROWS, S), lambda b, h: (b, h, 0)),
            pl.BlockSpec((1, nb, 1, pair_w), lambda b, h: (b, 0, 0, h)),
        ],
        out_specs=pl.BlockSpec((1, S, LANES), lambda b, h: (b, 0, h)),
        out_shape=jax.ShapeDtypeStruct((B, S, ATTN_WIDTH), _BF16),
        compiler_params=pltpu.CompilerParams(
            dimension_semantics=("parallel", "parallel"),
            vmem_limit_bytes=VMEM_LIMIT_BYTES),
        name="moba_attention",
    )(q, k, vt, km)


def _lru_kernel(x_ref, w_ref, cw_ref, cb_ref, wrg_ref, brg_ref, wig_ref, big_ref, lam_ref,
                o_ref, xbuf, a_s, u_s, hcar):
    t = pl.program_id(1)
    R = LRU_ROWS
    C = LRU_WIDTH
    PAD = SUBLANES

    @pl.when(t == 0)
    def _():
        xbuf[0:PAD, :] = jnp.zeros((PAD, C), _F32)
        hcar[...] = jnp.zeros_like(hcar)

    p = jnp.dot(x_ref[0].astype(_BF16), w_ref[...], preferred_element_type=_F32)
    xr = p[:, :C]
    gr = p[:, C:]

    xbuf[PAD:PAD + R, :] = xr
    y = cb_ref[...] + cw_ref[CONV_WIDTH - 1:CONV_WIDTH, :] * xr
    for d in range(1, CONV_WIDTH):
        y = y + cw_ref[CONV_WIDTH - 1 - d:CONV_WIDTH - d, :] * xbuf[PAD - d:PAD - d + R, :]
    xbuf[0:PAD, :] = xr[R - PAD:R, :]

    yb = y.astype(_BF16)
    half = C // 2

    def gate_lin(w3_ref, b_ref):
        parts = [jnp.dot(yb[:, s * half:(s + 1) * half], w3_ref[s], preferred_element_type=_F32)
                 for s in range(2)]
        return jnp.concatenate(parts, axis=1) + b_ref[...]

    r = jax.nn.sigmoid(gate_lin(wrg_ref, brg_ref))
    ig = jax.nn.sigmoid(gate_lin(wig_ref, big_ref))
    lam = lam_ref[...]
    softplus_neg_lam = jnp.maximum(-lam, 0.0) + jnp.log(1.0 + jnp.exp(-jnp.abs(lam)))
    log_a = (-LRU_C) * r * softplus_neg_lam
    a = jnp.exp(log_a)
    u = jnp.sqrt(1.0 - jnp.exp(2.0 * log_a)) * (ig * y)
    a_s[...] = a
    u_s[...] = u

    row = lax.broadcasted_iota(jnp.int32, (SUBLANES, C), 0)

    def group(g, carry):
        rows = pl.ds(pl.multiple_of(g * SUBLANES, SUBLANES), SUBLANES)
        ag = a_s[rows, :]
        ug = u_s[rows, :]
        for d in (1, 2, 4):
            a_sh = pltpu.roll(ag, d, 0)
            u_sh = pltpu.roll(ug, d, 0)
            m = row >= d
            ug = jnp.where(m, ag * u_sh + ug, ug)
            ag = jnp.where(m, ag * a_sh, ag)
        h = ag * carry + ug
        u_s[rows, :] = h
        return jnp.broadcast_to(h[SUBLANES - 1:SUBLANES, :], (SUBLANES, C))

    hcar[...] = lax.fori_loop(0, R // SUBLANES, group, hcar[...])

    gelu = 0.5 * gr * (1.0 + jnp.tanh(0.7978845608028654 * (gr + 0.044715 * (gr * gr * gr))))
    o_ref[0] = (u_s[...] * gelu).astype(o_ref.dtype)


def _lru_branch(x, w_lru, conv_w, conv_b, wrg, b_rg, wig, b_ig, lam):
    B, S, D = x.shape
    C = LRU_WIDTH
    full = lambda shape: pl.BlockSpec(shape, lambda b, t: (0,) * len(shape))
    return pl.pallas_call(
        _lru_kernel,
        grid=(B, S // LRU_ROWS),
        in_specs=[
            pl.BlockSpec((1, LRU_ROWS, D), lambda b, t: (b, t, 0)),
            full((D, 2 * C)),
            full((CONV_WIDTH, C)),
            full((1, C)),
            full((2, C // 2, C // 2)),
            full((1, C)),
            full((2, C // 2, C // 2)),
            full((1, C)),
            full((1, C)),
        ],
        out_specs=pl.BlockSpec((1, LRU_ROWS, C), lambda b, t: (b, t, 0)),
        out_shape=jax.ShapeDtypeStruct((B, S, C), _BF16),
        scratch_shapes=[
            pltpu.VMEM((LRU_ROWS + SUBLANES, C), _F32),
            pltpu.VMEM((LRU_ROWS, C), _F32),
            pltpu.VMEM((LRU_ROWS, C), _F32),
            pltpu.VMEM((SUBLANES, C), _F32),
        ],
        compiler_params=pltpu.CompilerParams(
            dimension_semantics=("parallel", "arbitrary"),
            vmem_limit_bytes=VMEM_LIMIT_BYTES),
        name="lru_branch",
    )(x, w_lru, conv_w, conv_b, wrg, b_rg, wig, b_ig, lam)


def _layer_norm(y, g, b):
    mu = jnp.mean(y, axis=-1, keepdims=True)
    yc = y - mu
    var = jnp.mean(yc * yc, axis=-1, keepdims=True)
    return yc * lax.rsqrt(var + LN_EPS) * g + b


def _mlp_kernel(x_ref, attn_ref, lru_ref, wo_ref, g1_ref, b1_ref, wup_ref, bup_ref,
                wdn_ref, bdn_ref, g2_ref, b2_ref, o_ref, x1_s, x1b_s, acc_s):
    f = pl.program_id(1)

    @pl.when(f == 0)
    def _():
        mix = jnp.dot(attn_ref[...], wo_ref[0:ATTN_WIDTH, :], preferred_element_type=_F32)
        mix = mix + jnp.dot(lru_ref[...], wo_ref[ATTN_WIDTH:, :], preferred_element_type=_F32)
        x1 = _layer_norm(DEEPNORM_ALPHA * x_ref[...] + mix, g1_ref[...], b1_ref[...])
        x1_s[...] = x1
        x1b_s[...] = x1.astype(_BF16)
        acc_s[...] = jnp.zeros_like(acc_s)

    h = jnp.dot(x1b_s[...], wup_ref[...], preferred_element_type=_F32) + bup_ref[...]
    h = jnp.maximum(h, 0.0)
    h = (h * h).astype(_BF16)
    acc_s[...] += jnp.dot(h, wdn_ref[...], preferred_element_type=_F32)

    @pl.when(f == pl.num_programs(1) - 1)
    def _():
        y = DEEPNORM_ALPHA * x1_s[...] + acc_s[...] + bdn_ref[...]
        o_ref[...] = _layer_norm(y, g2_ref[...], b2_ref[...])


def _outproj_mlp(x2, attn2, lru2, w_out, g1, b1, w_up, b_up, w_down, b_down, g2, b2):
    M, D = x2.shape
    F = w_up.shape[1]
    tm, tf = MLP_ROWS, MLP_FF_TILE
    row_vec = lambda n: pl.BlockSpec((1, n), lambda i, f: (0, 0))
    return pl.pallas_call(
        _mlp_kernel,
        grid=(M // tm, F // tf),
        in_specs=[
            pl.BlockSpec((tm, D), lambda i, f: (i, 0)),
            pl.BlockSpec((tm, ATTN_WIDTH), lambda i, f: (i, 0)),
            pl.BlockSpec((tm, LRU_WIDTH), lambda i, f: (i, 0)),
            pl.BlockSpec((D, D), lambda i, f: (0, 0)),
            row_vec(D), row_vec(D),
            pl.BlockSpec((D, tf), lambda i, f: (0, f)),
            pl.BlockSpec((1, tf), lambda i, f: (0, f)),
            pl.BlockSpec((tf, D), lambda i, f: (f, 0)),
            row_vec(D), row_vec(D), row_vec(D),
        ],
        out_specs=pl.BlockSpec((tm, D), lambda i, f: (i, 0)),
        out_shape=jax.ShapeDtypeStruct((M, D), _F32),
        scratch_shapes=[
            pltpu.VMEM((tm, D), _F32),
            pltpu.VMEM((tm, D), _BF16),
            pltpu.VMEM((tm, D), _F32),
        ],
        compiler_params=pltpu.CompilerParams(
            dimension_semantics=("parallel", "arbitrary"),
            vmem_limit_bytes=VMEM_LIMIT_BYTES),
        name="outproj_mlp",
    )(x2, attn2, lru2, w_out, g1, b1, w_up, b_up, w_down, b_down, g2, b2)


def _block_diag_halves(w):
    n_half = N_LRU_BLOCKS // 2
    eye = jnp.eye(n_half, dtype=w.dtype)
    wh = w.reshape(2, n_half, LRU_BLOCK, LRU_BLOCK)
    full = jnp.einsum('snde,nm->sndme', wh, eye)
    return full.reshape(2, n_half * LRU_BLOCK, n_half * LRU_BLOCK)


def kernel(x, w_in, conv_w, conv_b, w_rg, b_rg, w_ig, b_ig, lru_lambda, w_out, ln1_g, ln1_b,
           w_up, b_up, w_down, b_down, ln2_g, ln2_b):
    B, S, D = x.shape
    depth = w_in.shape[0]
    row = lambda v: v.reshape(1, -1)
    h = jnp.arange(1, N_HEADS + 1, dtype=_F32)
    slopes = jnp.exp2(-8.0 * h / N_HEADS)
    qx, kx = _aug_constants(S, slopes)
    for l in range(depth):
        w_qkv = w_in[l, :, :3 * ATTN_WIDTH].astype(_BF16)
        w_lru = w_in[l, :, 3 * ATTN_WIDTH:].astype(_BF16)
        q, k, vt, km = _qkv_proj(x, w_qkv, qx, kx)
        attn = _moba_attention(q, k, vt, km)
        lru = _lru_branch(x, w_lru, conv_w[l], row(conv_b[l]),
                          _block_diag_halves(w_rg[l]).astype(_BF16), row(b_rg[l]),
                          _block_diag_halves(w_ig[l]).astype(_BF16), row(b_ig[l]),
                          row(lru_lambda[l]))
        out = _outproj_mlp(
            x.reshape(B * S, D), attn.reshape(B * S, ATTN_WIDTH), lru.reshape(B * S, LRU_WIDTH),
            w_out[l].astype(_BF16), row(ln1_g[l]), row(ln1_b[l]),
            w_up[l].astype(_BF16), row(b_up[l]), w_down[l].astype(_BF16), row(b_down[l]),
            row(ln2_g[l]), row(ln2_b[l]))
        x = out.reshape(B, S, D)
    return x
```

```python
import numpy as np

import jax
import jax.numpy as jnp
from jax import lax
from jax.experimental import pallas as pl
from jax.experimental.pallas import tpu as pltpu

D_MODEL = 1024
ATTN_WIDTH = 512
N_HEADS = 8
HEAD_DIM = 64
LRU_WIDTH = 512
N_LRU_BLOCKS = 8
LRU_BLOCK = 64
CONV_WIDTH = 4
LRU_C = 8.0
MOBA_BLOCK = 256
MOBA_TOPK = 3
D_FF = 4096
LN_EPS = 1e-5
DEEPNORM_ALPHA = 2.0 ** 0.25
NEG_INF = -1e30

SUBLANES = 8
LANES = 128
HEADS_PER_LANE_TILE = LANES // HEAD_DIM
N_HEAD_PAIRS = N_HEADS // HEADS_PER_LANE_TILE
AUG_WIDTH = N_HEADS * LANES
SEL_LANE0 = HEAD_DIM
ALIBI_PIECES = 3
VT_ROWS = HEAD_DIM + 16
LOG2E = 1.4426950408889634

QKV_ROWS = 512
LRU_ROWS = 512
MLP_ROWS = 1024
MLP_FF_TILE = 512
VMEM_LIMIT_BYTES = 56 * 1024 * 1024

_F32 = jnp.float32
_BF16 = jnp.bfloat16
_NT_DIMS = (((1,), (1,)), ((), ()))


def _qkv_kernel(x_ref, w_ref, qx_ref, kx_ref, q_ref, k_ref, vt_ref, km_ref):
    R = QKV_ROWS
    xb = x_ref[0].astype(_BF16)
    p = jnp.dot(xb, w_ref[...], preferred_element_type=_F32)
    data_lane = lax.broadcasted_iota(jnp.int32, (R, LANES), 1) < HEAD_DIM
    qx = qx_ref[...]
    for pair in range(N_HEAD_PAIRS):
        q2 = p[:, pair * LANES:(pair + 1) * LANES] * (HEAD_DIM ** -0.5 * LOG2E)
        k2 = p[:, ATTN_WIDTH + pair * LANES:ATTN_WIDTH + (pair + 1) * LANES]
        for hh in range(HEADS_PER_LANE_TILE):
            cols = slice((pair * HEADS_PER_LANE_TILE + hh) * LANES,
                         (pair * HEADS_PER_LANE_TILE + hh + 1) * LANES)
            qh = q2 if hh == 0 else pltpu.roll(q2, HEAD_DIM, 1)
            kh = k2 if hh == 0 else pltpu.roll(k2, HEAD_DIM, 1)
            q_ref[0, :, cols] = jnp.where(data_lane, qh, qx).astype(_BF16)
            k_ref[0, :, cols] = jnp.where(data_lane, kh.astype(_BF16), kx_ref[:, cols])
            kpad = jnp.where(data_lane, kh, 0.0)
            for s in range(R // MOBA_BLOCK):
                km_ref[0, s, :, cols] = jnp.mean(kpad[s * MOBA_BLOCK:(s + 1) * MOBA_BLOCK],
                                                 axis=0, keepdims=True)
    vt = p[:, 2 * ATTN_WIDTH:].T.astype(_BF16)
    ones_rows = (lax.broadcasted_iota(jnp.int32, (VT_ROWS - HEAD_DIM, R), 0) == 0).astype(_BF16)
    for h in range(N_HEADS):
        vt_ref[0, h * VT_ROWS:h * VT_ROWS + HEAD_DIM, :] = vt[h * HEAD_DIM:(h + 1) * HEAD_DIM, :]
        vt_ref[0, h * VT_ROWS + HEAD_DIM:(h + 1) * VT_ROWS, :] = ones_rows


def _qkv_proj(x, w_qkv, qx, kx):
    B, S, D = x.shape
    nb = S // MOBA_BLOCK
    bps = QKV_ROWS // MOBA_BLOCK
    return pl.pallas_call(
        _qkv_kernel,
        grid=(B, S // QKV_ROWS),
        in_specs=[
            pl.BlockSpec((1, QKV_ROWS, D), lambda b, t: (b, t, 0)),
            pl.BlockSpec((D, 3 * ATTN_WIDTH), lambda b, t: (0, 0)),
            pl.BlockSpec((1, LANES), lambda b, t: (0, 0)),
            pl.BlockSpec((QKV_ROWS, AUG_WIDTH), lambda b, t: (t, 0)),
        ],
        out_specs=[
            pl.BlockSpec((1, QKV_ROWS, AUG_WIDTH), lambda b, t: (b, t, 0)),
            pl.BlockSpec((1, QKV_ROWS, AUG_WIDTH), lambda b, t: (b, t, 0)),
            pl.BlockSpec((1, N_HEADS * VT_ROWS, QKV_ROWS), lambda b, t: (b, 0, t)),
            pl.BlockSpec((1, bps, 1, AUG_WIDTH), lambda b, t: (b, t, 0, 0)),
        ],
        out_shape=[
            jax.ShapeDtypeStruct((B, S, AUG_WIDTH), _BF16),
            jax.ShapeDtypeStruct((B, S, AUG_WIDTH), _BF16),
            jax.ShapeDtypeStruct((B, N_HEADS * VT_ROWS, S), _BF16),
            jax.ShapeDtypeStruct((B, nb, 1, AUG_WIDTH), _F32),
        ],
        compiler_params=pltpu.CompilerParams(
            dimension_semantics=("parallel", "parallel"),
            vmem_limit_bytes=VMEM_LIMIT_BYTES),
        name="qkv_proj",
    )(x, w_qkv, qx, kx)


def _aug_constants(seq_len):
    nb = seq_len // MOBA_BLOCK
    alibi_lane0 = SEL_LANE0 + nb
    lane = np.arange(LANES)
    qx = ((lane >= alibi_lane0) & (lane < alibi_lane0 + ALIBI_PIECES)).astype(np.float32).reshape(1, LANES)

    slopes = np.exp2(-8.0 * np.arange(1, N_HEADS + 1, dtype=np.float32) / N_HEADS).astype(np.float32)
    pos = np.arange(seq_len)
    onehot = (pos[:, None] // MOBA_BLOCK == np.arange(nb)[None, :]).astype(np.float32)
    val = (np.float32(LOG2E) * slopes)[None, :] * (pos[:, None] - seq_len // 2).astype(np.float32)

    def trunc_bf16(v):
        return (v.view(np.uint32) & np.uint32(0xFFFF0000)).view(np.float32)

    hi = trunc_bf16(val)
    mid = trunc_bf16(val - hi)
    lo = val - hi - mid
    pieces = np.stack([hi, mid, lo], axis=-1)
    extras = np.concatenate([
        np.zeros((seq_len, N_HEADS, HEAD_DIM), np.float32),
        np.broadcast_to(onehot[:, None, :], (seq_len, N_HEADS, nb)),
        pieces,
        np.zeros((seq_len, N_HEADS, LANES - alibi_lane0 - ALIBI_PIECES), np.float32)], axis=-1)
    return jnp.asarray(qx), jnp.asarray(extras.reshape(seq_len, AUG_WIDTH).astype(_BF16))


def _attn_kernel(q_ref, k_ref, vt_ref, km_ref, o_ref):
    blk = MOBA_BLOCK
    nb = k_ref.shape[1] // blk

    lane = lax.broadcasted_iota(jnp.int32, (blk, LANES), 1)
    sel_lane = (lane >= SEL_LANE0) & (lane < SEL_LANE0 + nb)
    key_row = lax.broadcasted_iota(jnp.int32, (blk, blk), 0)
    qry_col = lax.broadcasted_iota(jnp.int32, (blk, blk), 1)
    causal = key_row <= qry_col
    blk_idx = lax.broadcasted_iota(jnp.int32, (nb, blk), 0)

    def scores(j, hh):
        nk = (j + 1) * blk
        cols = slice(hh * LANES, (hh + 1) * LANES)
        qa = q_ref[0, j * blk:nk, cols]
        if j > MOBA_TOPK:
            kmh = km_ref[0, :, 0, cols].astype(_BF16)
            gate = lax.dot_general(kmh, qa, _NT_DIMS, preferred_element_type=_F32)
            rank = jnp.zeros((nb, blk), jnp.int32)
            for n2 in range(j):
                g2 = gate[n2:n2 + 1, :]
                beats = (g2 > gate) | ((g2 == gate) & (n2 < blk_idx))
                rank = rank + beats.astype(jnp.int32)
            keep = ((blk_idx < j) & (rank < MOBA_TOPK)) | (blk_idx >= j)
            selb = jnp.where(keep, 0.0, NEG_INF)
            selb_rows = jnp.concatenate(
                [jnp.zeros((SEL_LANE0, blk), _F32), selb,
                 jnp.zeros((LANES - SEL_LANE0 - nb, blk), _F32)], axis=0)
            qa = jnp.where(sel_lane, selb_rows.T.astype(_BF16), qa)
        z = lax.dot_general(k_ref[0, 0:nk, cols], qa, _NT_DIMS,
                            preferred_element_type=_F32)
        z_own = jnp.where(causal, z[nk - blk:, :], NEG_INF)
        z = z_own if j == 0 else jnp.concatenate([z[:nk - blk, :], z_own], axis=0)
        m = jnp.max(z, axis=0, keepdims=True)
        return z, m

    def probs(zm):
        z, m = zm
        return jnp.exp2(z - m).astype(_BF16)

    def weighted_values(j, hh, p):
        nk = (j + 1) * blk
        vt = vt_ref[0, hh * VT_ROWS:(hh + 1) * VT_ROWS, 0:nk]
        acc = jnp.dot(vt, p, preferred_element_type=_F32)
        return acc[0:HEAD_DIM, :] / acc[HEAD_DIM:HEAD_DIM + 1, :]

    items = [(j, hh) for j in range(nb) for hh in range(HEADS_PER_LANE_TILE)]
    n_items = len(items)
    zm = {0: scores(*items[0])}
    pls = {}
    outs = {}
    for t in range(n_items + 1):
        if t + 1 < n_items:
            zm[t + 1] = scores(*items[t + 1])
        if t < n_items:
            pls[t] = probs(zm.pop(t))
        if t >= 1:
            j, hh = items[t - 1]
            outs[hh] = weighted_values(j, hh, pls.pop(t - 1))
            if hh == HEADS_PER_LANE_TILE - 1:
                o_ref[0, j * blk:(j + 1) * blk, :] = jnp.concatenate(
                    [outs[h] for h in range(HEADS_PER_LANE_TILE)], axis=0).T.astype(o_ref.dtype)


def _moba_attention(q, k, vt, km):
    B, S, _ = q.shape
    nb = S // MOBA_BLOCK
    pair_w = HEADS_PER_LANE_TILE * LANES
    return pl.pallas_call(
        _attn_kernel,
        grid=(B, N_HEAD_PAIRS),
        in_specs=[
            pl.BlockSpec((1, S, pair_w), lambda b, h: (b, 0, h)),
            pl.BlockSpec((1, S, pair_w), lambda b, h: (b, 0, h)),
            pl.BlockSpec((1, HEADS_PER_LANE_TILE * VT_ROWS, S), lambda b, h: (b, h, 0)),
            pl.BlockSpec((1, nb, 1, pair_w), lambda b, h: (b, 0, 0, h)),
        ],
        out_specs=pl.BlockSpec((1, S, LANES), lambda b, h: (b, 0, h)),
        out_shape=jax.ShapeDtypeStruct((B, S, ATTN_WIDTH), _BF16),
        compiler_params=pltpu.CompilerParams(
            dimension_semantics=("parallel", "parallel"),
            vmem_limit_bytes=VMEM_LIMIT_BYTES),
        name="moba_attention",
    )(q, k, vt, km)


def _lru_kernel(x_ref, w_ref, cw_ref, cb_ref, wrg_ref, brg_ref, wig_ref, big_ref, lam_ref,
                o_ref, xbuf, a_s, u_s, hcar):
    t = pl.program_id(1)
    R = LRU_ROWS
    C = LRU_WIDTH
    PAD = SUBLANES

    @pl.when(t == 0)
    def _():
        xbuf[0:PAD, :] = jnp.zeros((PAD, C), _F32)
        hcar[...] = jnp.zeros_like(hcar)

    p = jnp.dot(x_ref[0].astype(_BF16), w_ref[...], preferred_element_type=_F32)
    xr = p[:, :C]
    gr = p[:, C:]

    xbuf[PAD:PAD + R, :] = xr
    y = cb_ref[...] + cw_ref[CONV_WIDTH - 1:CONV_WIDTH, :] * xr
    for d in range(1, CONV_WIDTH):
        y = y + cw_ref[CONV_WIDTH - 1 - d:CONV_WIDTH - d, :] * xbuf[PAD - d:PAD - d + R, :]
    xbuf[0:PAD, :] = xr[R - PAD:R, :]

    yb = y.astype(_BF16)
    half = C // 2

    def gate_lin(w3_ref, b_ref):
        parts = [jnp.dot(yb[:, s * half:(s + 1) * half], w3_ref[s], preferred_element_type=_F32)
                 for s in range(2)]
        return jnp.concatenate(parts, axis=1) + b_ref[...]

    def sigmoid(v):
        return 0.5 * jnp.tanh(0.5 * v) + 0.5

    r = sigmoid(gate_lin(wrg_ref, brg_ref))
    ig = sigmoid(gate_lin(wig_ref, big_ref))
    lam = lam_ref[...]
    softplus_neg_lam = jnp.maximum(-lam, 0.0) + jnp.log(1.0 + jnp.exp(-jnp.abs(lam)))
    a = jnp.exp(r * ((-LRU_C) * softplus_neg_lam))
    v = 1.0 - a * a
    u = jnp.where(v > 0.0, v * lax.rsqrt(v), 0.0) * (ig * y)
    a_s[...] = a
    u_s[...] = u

    row = lax.broadcasted_iota(jnp.int32, (SUBLANES, C), 0)

    def group(g, carry):
        rows = pl.ds(pl.multiple_of(g * SUBLANES, SUBLANES), SUBLANES)
        ag = a_s[rows, :]
        ug = u_s[rows, :]
        for d in (1, 2, 4):
            a_sh = pltpu.roll(ag, d, 0)
            u_sh = pltpu.roll(ug, d, 0)
            m = row >= d
            ug = jnp.where(m, ag * u_sh + ug, ug)
            ag = jnp.where(m, ag * a_sh, ag)
        h = ag * carry + ug
        u_s[rows, :] = h
        return jnp.broadcast_to(h[SUBLANES - 1:SUBLANES, :], (SUBLANES, C))

    hcar[...] = lax.fori_loop(0, R // SUBLANES, group, hcar[...])

    gelu = 0.5 * gr * (1.0 + jnp.tanh(0.7978845608028654 * (gr + 0.044715 * (gr * gr * gr))))
    o_ref[0] = (u_s[...] * gelu).astype(o_ref.dtype)


def _lru_branch(x, w_lru, conv_w, conv_b, wrg, b_rg, wig, b_ig, lam):
    B, S, D = x.shape
    C = LRU_WIDTH
    full = lambda shape: pl.BlockSpec(shape, lambda b, t: (0,) * len(shape))
    return pl.pallas_call(
        _lru_kernel,
        grid=(B, S // LRU_ROWS),
        in_specs=[
            pl.BlockSpec((1, LRU_ROWS, D), lambda b, t: (b, t, 0)),
            full((D, 2 * C)),
            full((CONV_WIDTH, C)),
            full((1, C)),
            full((2, C // 2, C // 2)),
            full((1, C)),
            full((2, C // 2, C // 2)),
            full((1, C)),
            full((1, C)),
        ],
        out_specs=pl.BlockSpec((1, LRU_ROWS, C), lambda b, t: (b, t, 0)),
        out_shape=jax.ShapeDtypeStruct((B, S, C), _BF16),
        scratch_shapes=[
            pltpu.VMEM((LRU_ROWS + SUBLANES, C), _F32),
            pltpu.VMEM((LRU_ROWS, C), _F32),
            pltpu.VMEM((LRU_ROWS, C), _F32),
            pltpu.VMEM((SUBLANES, C), _F32),
        ],
        compiler_params=pltpu.CompilerParams(
            dimension_semantics=("parallel", "arbitrary"),
            vmem_limit_bytes=VMEM_LIMIT_BYTES),
        name="lru_branch",
    )(x, w_lru, conv_w, conv_b, wrg, b_rg, wig, b_ig, lam)


def _layer_norm(y, g, b):
    mu = jnp.mean(y, axis=-1, keepdims=True)
    yc = y - mu
    var = jnp.mean(yc * yc, axis=-1, keepdims=True)
    return yc * lax.rsqrt(var + LN_EPS) * g + b


def _mlp_kernel(x_ref, attn_ref, lru_ref, wo_ref, g1_ref, b1_ref, wup_ref, bup_ref,
                wdn_ref, bdn_ref, g2_ref, b2_ref, o_ref, x1_s, x1b_s, acc_s):
    f = pl.program_id(1)

    @pl.when(f == 0)
    def _():
        mix = jnp.dot(attn_ref[...], wo_ref[0:ATTN_WIDTH, :], preferred_element_type=_F32)
        mix = mix + jnp.dot(lru_ref[...], wo_ref[ATTN_WIDTH:, :], preferred_element_type=_F32)
        x1 = _layer_norm(DEEPNORM_ALPHA * x_ref[...] + mix, g1_ref[...], b1_ref[...])
        x1_s[...] = x1
        x1b_s[...] = x1.astype(_BF16)
        acc_s[...] = jnp.zeros_like(acc_s)

    h = jnp.dot(x1b_s[...], wup_ref[...], preferred_element_type=_F32) + bup_ref[...]
    h = jnp.maximum(h, 0.0)
    h = (h * h).astype(_BF16)
    acc_s[...] += jnp.dot(h, wdn_ref[...], preferred_element_type=_F32)

    @pl.when(f == pl.num_programs(1) - 1)
    def _():
        y = DEEPNORM_ALPHA * x1_s[...] + acc_s[...] + bdn_ref[...]
        o_ref[...] = _layer_norm(y, g2_ref[...], b2_ref[...])


def _outproj_mlp(x2, attn2, lru2, w_out, g1, b1, w_up, b_up, w_down, b_down, g2, b2):
    M, D = x2.shape
    F = w_up.shape[1]
    tm, tf = MLP_ROWS, MLP_FF_TILE
    row_vec = lambda n: pl.BlockSpec((1, n), lambda i, f: (0, 0))
    return pl.pallas_call(
        _mlp_kernel,
        grid=(M // tm, F // tf),
        in_specs=[
            pl.BlockSpec((tm, D), lambda i, f: (i, 0)),
            pl.BlockSpec((tm, ATTN_WIDTH), lambda i, f: (i, 0)),
            pl.BlockSpec((tm, LRU_WIDTH), lambda i, f: (i, 0)),
            pl.BlockSpec((D, D), lambda i, f: (0, 0)),
            row_vec(D), row_vec(D),
            pl.BlockSpec((D, tf), lambda i, f: (0, f)),
            pl.BlockSpec((1, tf), lambda i, f: (0, f)),
            pl.BlockSpec((tf, D), lambda i, f: (f, 0)),
            row_vec(D), row_vec(D), row_vec(D),
        ],
        out_specs=pl.BlockSpec((tm, D), lambda i, f: (i, 0)),
        out_shape=jax.ShapeDtypeStruct((M, D), _F32),
        scratch_shapes=[
            pltpu.VMEM((tm, D), _F32),
            pltpu.VMEM((tm, D), _BF16),
            pltpu.VMEM((tm, D), _F32),
        ],
        compiler_params=pltpu.CompilerParams(
            dimension_semantics=("parallel", "arbitrary"),
            vmem_limit_bytes=VMEM_LIMIT_BYTES),
        name="outproj_mlp",
    )(x2, attn2, lru2, w_out, g1, b1, w_up, b_up, w_down, b_down, g2, b2)


def _block_diag_halves(w):
    n_half = N_LRU_BLOCKS // 2
    eye = jnp.eye(n_half, dtype=w.dtype)
    wh = w.reshape(2, n_half, LRU_BLOCK, LRU_BLOCK)
    full = jnp.einsum('snde,nm->sndme', wh, eye)
    return full.reshape(2, n_half * LRU_BLOCK, n_half * LRU_BLOCK)


def kernel(x, w_in, conv_w, conv_b, w_rg, b_rg, w_ig, b_ig, lru_lambda, w_out, ln1_g, ln1_b,
           w_up, b_up, w_down, b_down, ln2_g, ln2_b):
    B, S, D = x.shape
    depth = w_in.shape[0]
    row = lambda v: v.reshape(1, -1)
    qx, kx = _aug_constants(S)
    for l in range(depth):
        w_qkv = w_in[l, :, :3 * ATTN_WIDTH].astype(_BF16)
        w_lru = w_in[l, :, 3 * ATTN_WIDTH:].astype(_BF16)
        q, k, vt, km = _qkv_proj(x, w_qkv, qx, kx)
        attn = _moba_attention(q, k, vt, km)
        lru = _lru_branch(x, w_lru, conv_w[l], row(conv_b[l]),
                          _block_diag_halves(w_rg[l]).astype(_BF16), row(b_rg[l]),
                          _block_diag_halves(w_ig[l]).astype(_BF16), row(b_ig[l]),
                          row(lru_lambda[l]))
        out = _outproj_mlp(
            x.reshape(B * S, D), attn.reshape(B * S, ATTN_WIDTH), lru.reshape(B * S, LRU_WIDTH),
            w_out[l].astype(_BF16), row(ln1_g[l]), row(ln1_b[l]),
            w_up[l].astype(_BF16), row(b_up[l]), w_down[l].astype(_BF16), row(b_down[l]),
            row(ln2_g[l]), row(ln2_b[l]))
        x = out.reshape(B, S, D)
    return x
```

```python
import numpy as np

import jax
import jax.numpy as jnp
from jax import lax
from jax.experimental import pallas as pl
from jax.experimental.pallas import tpu as pltpu

D_MODEL = 1024
ATTN_WIDTH = 512
N_HEADS = 8
HEAD_DIM = 64
LRU_WIDTH = 512
N_LRU_BLOCKS = 8
LRU_BLOCK = 64
CONV_WIDTH = 4
LRU_C = 8.0
MOBA_BLOCK = 256
MOBA_TOPK = 3
D_FF = 4096
LN_EPS = 1e-5
DEEPNORM_ALPHA = 2.0 ** 0.25
NEG_INF = -1e30

SUBLANES = 8
LANES = 128
HEADS_PER_LANE_TILE = LANES // HEAD_DIM
N_HEAD_PAIRS = N_HEADS // HEADS_PER_LANE_TILE
AUG_WIDTH = N_HEADS * LANES
SEL_LANE0 = HEAD_DIM
ALIBI_PIECES = 3
VT_ROWS = HEAD_DIM + 16
LOG2E = 1.4426950408889634

QKV_ROWS = 512
LRU_ROWS = 512
MLP_ROWS = 512
MLP_SUB_ROWS = 256
MLP_FF_TILE = 512
VMEM_LIMIT_BYTES = 56 * 1024 * 1024

_F32 = jnp.float32
_BF16 = jnp.bfloat16
_NT_DIMS = (((1,), (1,)), ((), ()))


def _qkv_kernel(x_ref, w_ref, qx_ref, kx_ref, q_ref, k_ref, vt_ref, km_ref):
    R = QKV_ROWS
    xb = x_ref[0].astype(_BF16)
    p = jnp.dot(xb, w_ref[...], preferred_element_type=_F32)
    data_lane = lax.broadcasted_iota(jnp.int32, (R, LANES), 1) < HEAD_DIM
    qx = qx_ref[...]
    for pair in range(N_HEAD_PAIRS):
        q2 = p[:, pair * LANES:(pair + 1) * LANES] * (HEAD_DIM ** -0.5 * LOG2E)
        k2 = p[:, ATTN_WIDTH + pair * LANES:ATTN_WIDTH + (pair + 1) * LANES]
        for hh in range(HEADS_PER_LANE_TILE):
            cols = slice((pair * HEADS_PER_LANE_TILE + hh) * LANES,
                         (pair * HEADS_PER_LANE_TILE + hh + 1) * LANES)
            qh = q2 if hh == 0 else pltpu.roll(q2, HEAD_DIM, 1)
            kh = k2 if hh == 0 else pltpu.roll(k2, HEAD_DIM, 1)
            q_ref[0, :, cols] = jnp.where(data_lane, qh, qx).astype(_BF16)
            k_ref[0, :, cols] = jnp.where(data_lane, kh.astype(_BF16), kx_ref[:, cols])
            kpad = jnp.where(data_lane, kh, 0.0)
            for s in range(R // MOBA_BLOCK):
                km_ref[0, s, :, cols] = jnp.mean(kpad[s * MOBA_BLOCK:(s + 1) * MOBA_BLOCK],
                                                 axis=0, keepdims=True)
    vt = p[:, 2 * ATTN_WIDTH:].T.astype(_BF16)
    ones_rows = (lax.broadcasted_iota(jnp.int32, (VT_ROWS - HEAD_DIM, R), 0) == 0).astype(_BF16)
    for h in range(N_HEADS):
        vt_ref[0, h * VT_ROWS:h * VT_ROWS + HEAD_DIM, :] = vt[h * HEAD_DIM:(h + 1) * HEAD_DIM, :]
        vt_ref[0, h * VT_ROWS + HEAD_DIM:(h + 1) * VT_ROWS, :] = ones_rows


def _qkv_proj(x, w_qkv, qx, kx):
    B, S, D = x.shape
    nb = S // MOBA_BLOCK
    bps = QKV_ROWS // MOBA_BLOCK
    return pl.pallas_call(
        _qkv_kernel,
        grid=(B, S // QKV_ROWS),
        in_specs=[
            pl.BlockSpec((1, QKV_ROWS, D), lambda b, t: (b, t, 0)),
            pl.BlockSpec((D, 3 * ATTN_WIDTH), lambda b, t: (0, 0)),
            pl.BlockSpec((1, LANES), lambda b, t: (0, 0)),
            pl.BlockSpec((QKV_ROWS, AUG_WIDTH), lambda b, t: (t, 0)),
        ],
        out_specs=[
            pl.BlockSpec((1, QKV_ROWS, AUG_WIDTH), lambda b, t: (b, t, 0)),
            pl.BlockSpec((1, QKV_ROWS, AUG_WIDTH), lambda b, t: (b, t, 0)),
            pl.BlockSpec((1, N_HEADS * VT_ROWS, QKV_ROWS), lambda b, t: (b, 0, t)),
            pl.BlockSpec((1, bps, 1, AUG_WIDTH), lambda b, t: (b, t, 0, 0)),
        ],
        out_shape=[
            jax.ShapeDtypeStruct((B, S, AUG_WIDTH), _BF16),
            jax.ShapeDtypeStruct((B, S, AUG_WIDTH), _BF16),
            jax.ShapeDtypeStruct((B, N_HEADS * VT_ROWS, S), _BF16),
            jax.ShapeDtypeStruct((B, nb, 1, AUG_WIDTH), _F32),
        ],
        compiler_params=pltpu.CompilerParams(
            dimension_semantics=("parallel", "parallel"),
            vmem_limit_bytes=VMEM_LIMIT_BYTES),
        name="qkv_proj",
    )(x, w_qkv, qx, kx)


def _aug_constants(seq_len):
    nb = seq_len // MOBA_BLOCK
    alibi_lane0 = SEL_LANE0 + nb
    lane = np.arange(LANES)
    qx = ((lane >= alibi_lane0) & (lane < alibi_lane0 + ALIBI_PIECES)).astype(np.float32).reshape(1, LANES)

    slopes = np.exp2(-8.0 * np.arange(1, N_HEADS + 1, dtype=np.float32) / N_HEADS).astype(np.float32)
    pos = np.arange(seq_len)
    onehot = (pos[:, None] // MOBA_BLOCK == np.arange(nb)[None, :]).astype(np.float32)
    val = (np.float32(LOG2E) * slopes)[None, :] * (pos[:, None] - seq_len // 2).astype(np.float32)

    def trunc_bf16(v):
        return (v.view(np.uint32) & np.uint32(0xFFFF0000)).view(np.float32)

    hi = trunc_bf16(val)
    mid = trunc_bf16(val - hi)
    lo = val - hi - mid
    pieces = np.stack([hi, mid, lo], axis=-1)
    extras = np.concatenate([
        np.zeros((seq_len, N_HEADS, HEAD_DIM), np.float32),
        np.broadcast_to(onehot[:, None, :], (seq_len, N_HEADS, nb)),
        pieces,
        np.zeros((seq_len, N_HEADS, LANES - alibi_lane0 - ALIBI_PIECES), np.float32)], axis=-1)
    return jnp.asarray(qx), jnp.asarray(extras.reshape(seq_len, AUG_WIDTH).astype(_BF16))


def _attn_kernel(q_ref, k_ref, vt_ref, km_ref, o_ref):
    blk = MOBA_BLOCK
    nb = k_ref.shape[1] // blk

    lane = lax.broadcasted_iota(jnp.int32, (blk, LANES), 1)
    sel_lane = (lane >= SEL_LANE0) & (lane < SEL_LANE0 + nb)
    key_row = lax.broadcasted_iota(jnp.int32, (blk, blk), 0)
    qry_col = lax.broadcasted_iota(jnp.int32, (blk, blk), 1)
    causal = key_row <= qry_col
    blk_idx = lax.broadcasted_iota(jnp.int32, (nb, blk), 0)

    def scores(j, hh):
        nk = (j + 1) * blk
        cols = slice(hh * LANES, (hh + 1) * LANES)
        qa = q_ref[0, j * blk:nk, cols]
        if j > MOBA_TOPK:
            kmh = km_ref[0, :, 0, cols].astype(_BF16)
            gate = lax.dot_general(kmh, qa, _NT_DIMS, preferred_element_type=_F32)
            rank = jnp.zeros((nb, blk), jnp.int32)
            for n2 in range(j):
                g2 = gate[n2:n2 + 1, :]
                beats = (g2 > gate) | ((g2 == gate) & (n2 < blk_idx))
                rank = rank + beats.astype(jnp.int32)
            keep = ((blk_idx < j) & (rank < MOBA_TOPK)) | (blk_idx >= j)
            selb = jnp.where(keep, 0.0, NEG_INF)
            selb_rows = jnp.concatenate(
                [jnp.zeros((SEL_LANE0, blk), _F32), selb,
                 jnp.zeros((LANES - SEL_LANE0 - nb, blk), _F32)], axis=0)
            qa = jnp.where(sel_lane, selb_rows.T.astype(_BF16), qa)
        z = lax.dot_general(k_ref[0, 0:nk, cols], qa, _NT_DIMS,
                            preferred_element_type=_F32)
        z_own = jnp.where(causal, z[nk - blk:, :], NEG_INF)
        z = z_own if j == 0 else jnp.concatenate([z[:nk - blk, :], z_own], axis=0)
        m = jnp.max(z, axis=0, keepdims=True)
        return z, m

    def probs(zm):
        z, m = zm
        return jnp.exp2(z - m).astype(_BF16)

    def weighted_values(j, hh, p):
        nk = (j + 1) * blk
        vt = vt_ref[0, hh * VT_ROWS:(hh + 1) * VT_ROWS, 0:nk]
        acc = jnp.dot(vt, p, preferred_element_type=_F32)
        return acc[0:HEAD_DIM, :] / acc[HEAD_DIM:HEAD_DIM + 1, :]

    items = [(j, hh) for j in range(nb) for hh in range(HEADS_PER_LANE_TILE)]
    n_items = len(items)
    zm = {0: scores(*items[0])}
    pls = {}
    outs = {}
    for t in range(n_items + 1):
        if t + 1 < n_items:
            zm[t + 1] = scores(*items[t + 1])
        if t < n_items:
            pls[t] = probs(zm.pop(t))
        if t >= 1:
            j, hh = items[t - 1]
            outs[hh] = weighted_values(j, hh, pls.pop(t - 1))
            if hh == HEADS_PER_LANE_TILE - 1:
                o_ref[0, j * blk:(j + 1) * blk, :] = jnp.concatenate(
                    [outs[h] for h in range(HEADS_PER_LANE_TILE)], axis=0).T.astype(o_ref.dtype)


def _moba_attention(q, k, vt, km):
    B, S, _ = q.shape
    nb = S // MOBA_BLOCK
    pair_w = HEADS_PER_LANE_TILE * LANES
    return pl.pallas_call(
        _attn_kernel,
        grid=(B, N_HEAD_PAIRS),
        in_specs=[
            pl.BlockSpec((1, S, pair_w), lambda b, h: (b, 0, h)),
            pl.BlockSpec((1, S, pair_w), lambda b, h: (b, 0, h)),
            pl.BlockSpec((1, HEADS_PER_LANE_TILE * VT_ROWS, S), lambda b, h: (b, h, 0)),
            pl.BlockSpec((1, nb, 1, pair_w), lambda b, h: (b, 0, 0, h)),
        ],
        out_specs=pl.BlockSpec((1, S, LANES), lambda b, h: (b, 0, h)),
        out_shape=jax.ShapeDtypeStruct((B, S, ATTN_WIDTH), _BF16),
        compiler_params=pltpu.CompilerParams(
            dimension_semantics=("parallel", "parallel"),
            vmem_limit_bytes=VMEM_LIMIT_BYTES),
        name="moba_attention",
    )(q, k, vt, km)


def _lru_kernel(x_ref, w_ref, cw_ref, cb_ref, wrg_ref, brg_ref, wig_ref, big_ref, lam_ref,
                o_ref, xbuf, a_s, u_s, hcar):
    t = pl.program_id(1)
    R = LRU_ROWS
    C = LRU_WIDTH
    PAD = SUBLANES

    @pl.when(t == 0)
    def _():
        xbuf[0:PAD, :] = jnp.zeros((PAD, C), _F32)
        hcar[...] = jnp.zeros_like(hcar)

    p = jnp.dot(x_ref[0].astype(_BF16), w_ref[...], preferred_element_type=_F32)
    xr = p[:, :C]
    gr = p[:, C:]

    xbuf[PAD:PAD + R, :] = xr
    y = cb_ref[...] + cw_ref[CONV_WIDTH - 1:CONV_WIDTH, :] * xr
    for d in range(1, CONV_WIDTH):
        y = y + cw_ref[CONV_WIDTH - 1 - d:CONV_WIDTH - d, :] * xbuf[PAD - d:PAD - d + R, :]
    xbuf[0:PAD, :] = xr[R - PAD:R, :]

    yb = y.astype(_BF16)
    half = C // 2

    def gate_lin(w3_ref, b_ref):
        parts = [jnp.dot(yb[:, s * half:(s + 1) * half], w3_ref[s], preferred_element_type=_F32)
                 for s in range(2)]
        return jnp.concatenate(parts, axis=1) + b_ref[...]

    def sigmoid(v):
        return 0.5 * jnp.tanh(0.5 * v) + 0.5

    r = sigmoid(gate_lin(wrg_ref, brg_ref))
    ig = sigmoid(gate_lin(wig_ref, big_ref))
    lam = lam_ref[...]
    softplus_neg_lam = jnp.maximum(-lam, 0.0) + jnp.log(1.0 + jnp.exp(-jnp.abs(lam)))
    a = jnp.exp(r * ((-LRU_C) * softplus_neg_lam))
    v = 1.0 - a * a
    u = jnp.where(v > 0.0, v * lax.rsqrt(v), 0.0) * (ig * y)
    a_s[...] = a
    u_s[...] = u

    row = lax.broadcasted_iota(jnp.int32, (SUBLANES, C), 0)

    def group(g, carry):
        rows = pl.ds(pl.multiple_of(g * SUBLANES, SUBLANES), SUBLANES)
        ag = a_s[rows, :]
        ug = u_s[rows, :]
        for d in (1, 2, 4):
            a_sh = pltpu.roll(ag, d, 0)
            u_sh = pltpu.roll(ug, d, 0)
            m = row >= d
            ug = jnp.where(m, ag * u_sh + ug, ug)
            ag = jnp.where(m, ag * a_sh, ag)
        h = ag * carry + ug
        u_s[rows, :] = h
        return jnp.broadcast_to(h[SUBLANES - 1:SUBLANES, :], (SUBLANES, C))

    hcar[...] = lax.fori_loop(0, R // SUBLANES, group, hcar[...])

    gelu = 0.5 * gr * (1.0 + jnp.tanh(0.7978845608028654 * (gr + 0.044715 * (gr * gr * gr))))
    o_ref[0] = (u_s[...] * gelu).astype(o_ref.dtype)


def _lru_branch(x, w_lru, conv_w, conv_b, wrg, b_rg, wig, b_ig, lam):
    B, S, D = x.shape
    C = LRU_WIDTH
    full = lambda shape: pl.BlockSpec(shape, lambda b, t: (0,) * len(shape))
    return pl.pallas_call(
        _lru_kernel,
        grid=(B, S // LRU_ROWS),
        in_specs=[
            pl.BlockSpec((1, LRU_ROWS, D), lambda b, t: (b, t, 0)),
            full((D, 2 * C)),
            full((CONV_WIDTH, C)),
            full((1, C)),
            full((2, C // 2, C // 2)),
            full((1, C)),
            full((2, C // 2, C // 2)),
            full((1, C)),
            full((1, C)),
        ],
        out_specs=pl.BlockSpec((1, LRU_ROWS, C), lambda b, t: (b, t, 0)),
        out_shape=jax.ShapeDtypeStruct((B, S, C), _BF16),
        scratch_shapes=[
            pltpu.VMEM((LRU_ROWS + SUBLANES, C), _F32),
            pltpu.VMEM((LRU_ROWS, C), _F32),
            pltpu.VMEM((LRU_ROWS, C), _F32),
            pltpu.VMEM((SUBLANES, C), _F32),
        ],
        compiler_params=pltpu.CompilerParams(
            dimension_semantics=("parallel", "arbitrary"),
            vmem_limit_bytes=VMEM_LIMIT_BYTES),
        name="lru_branch",
    )(x, w_lru, conv_w, conv_b, wrg, b_rg, wig, b_ig, lam)


def _layer_norm(y, g, b):
    mu = jnp.mean(y, axis=-1, keepdims=True)
    yc = y - mu
    var = jnp.mean(yc * yc, axis=-1, keepdims=True)
    return yc * lax.rsqrt(var + LN_EPS) * g + b


def _mlp_kernel(x_ref, attn_ref, lru_ref, wo_ref, g1_ref, b1_ref, wup_ref, bup_ref,
                wdn_ref, bdn_ref, g2_ref, b2_ref, o_ref):
    n_sub = MLP_ROWS // MLP_SUB_ROWS
    n_ff = wup_ref.shape[1] // MLP_FF_TILE

    def rows(c):
        return slice(c * MLP_SUB_ROWS, (c + 1) * MLP_SUB_ROWS)

    def out_proj(c):
        mix = jnp.dot(attn_ref[rows(c), :], wo_ref[0:ATTN_WIDTH, :], preferred_element_type=_F32)
        return mix + jnp.dot(lru_ref[rows(c), :], wo_ref[ATTN_WIDTH:, :], preferred_element_type=_F32)

    def norm1(c, mix):
        x1 = _layer_norm(DEEPNORM_ALPHA * x_ref[rows(c), :] + mix, g1_ref[...], b1_ref[...])
        return x1, x1.astype(_BF16)

    def ff_tile(x1b, f):
        cols = slice(f * MLP_FF_TILE, (f + 1) * MLP_FF_TILE)
        h = jnp.dot(x1b, wup_ref[:, cols], preferred_element_type=_F32) + bup_ref[:, cols]
        h = jnp.maximum(h, 0.0)
        return jnp.dot((h * h).astype(_BF16), wdn_ref[cols, :], preferred_element_type=_F32)

    def norm2(c, x1, acc):
        y = DEEPNORM_ALPHA * x1 + acc + bdn_ref[...]
        o_ref[rows(c), :] = _layer_norm(y, g2_ref[...], b2_ref[...])

    x1, x1b = norm1(0, out_proj(0))
    pending = None
    for c in range(n_sub):
        nxt_mix = out_proj(c + 1) if c + 1 < n_sub else None
        nxt = None
        acc = None
        for f in range(n_ff):
            part = ff_tile(x1b, f)
            acc = part if acc is None else acc + part
            if f == 1 and nxt_mix is not None:
                nxt = norm1(c + 1, nxt_mix)
            if f == n_ff // 2 and pending is not None:
                norm2(*pending)
                pending = None
        pending = (c, x1, acc)
        if nxt is not None:
            x1, x1b = nxt
    norm2(*pending)


def _outproj_mlp(x2, attn2, lru2, w_out, g1, b1, w_up, b_up, w_down, b_down, g2, b2):
    M, D = x2.shape
    F = w_up.shape[1]
    tm = MLP_ROWS
    resident = lambda shape: pl.BlockSpec(shape, lambda i: (0, 0), pipeline_mode=pl.Buffered(1))
    return pl.pallas_call(
        _mlp_kernel,
        grid=(M // tm,),
        in_specs=[
            pl.BlockSpec((tm, D), lambda i: (i, 0)),
            pl.BlockSpec((tm, ATTN_WIDTH), lambda i: (i, 0)),
            pl.BlockSpec((tm, LRU_WIDTH), lambda i: (i, 0)),
            resident((D, D)),
            resident((1, D)), resident((1, D)),
            resident((D, F)),
            resident((1, F)),
            resident((F, D)),
            resident((1, D)), resident((1, D)), resident((1, D)),
        ],
        out_specs=pl.BlockSpec((tm, D), lambda i: (i, 0)),
        out_shape=jax.ShapeDtypeStruct((M, D), _F32),
        compiler_params=pltpu.CompilerParams(
            dimension_semantics=("parallel",),
            vmem_limit_bytes=VMEM_LIMIT_BYTES),
        name="outproj_mlp",
    )(x2, attn2, lru2, w_out, g1, b1, w_up, b_up, w_down, b_down, g2, b2)


def _block_diag_halves(w):
    n_half = N_LRU_BLOCKS // 2
    eye = jnp.eye(n_half, dtype=w.dtype)
    wh = w.reshape(2, n_half, LRU_BLOCK, LRU_BLOCK)
    full = jnp.einsum('snde,nm->sndme', wh, eye)
    return full.reshape(2, n_half * LRU_BLOCK, n_half * LRU_BLOCK)


def kernel(x, w_in, conv_w, conv_b, w_rg, b_rg, w_ig, b_ig, lru_lambda, w_out, ln1_g, ln1_b,
           w_up, b_up, w_down, b_down, ln2_g, ln2_b):
    B, S, D = x.shape
    depth = w_in.shape[0]
    row = lambda v: v.reshape(1, -1)
    qx, kx = _aug_constants(S)
    for l in range(depth):
        w_qkv = w_in[l, :, :3 * ATTN_WIDTH].astype(_BF16)
        w_lru = w_in[l, :, 3 * ATTN_WIDTH:].astype(_BF16)
        q, k, vt, km = _qkv_proj(x, w_qkv, qx, kx)
        attn = _moba_attention(q, k, vt, km)
        lru = _lru_branch(x, w_lru, conv_w[l], row(conv_b[l]),
                          _block_diag_halves(w_rg[l]).astype(_BF16), row(b_rg[l]),
                          _block_diag_halves(w_ig[l]).astype(_BF16), row(b_ig[l]),
                          row(lru_lambda[l]))
        out = _outproj_mlp(
            x.reshape(B * S, D), attn.reshape(B * S, ATTN_WIDTH), lru.reshape(B * S, LRU_WIDTH),
            w_out[l].astype(_BF16), row(ln1_g[l]), row(ln1_b[l]),
            w_up[l].astype(_BF16), row(b_up[l]), w_down[l].astype(_BF16), row(b_down[l]),
            row(ln2_g[l]), row(ln2_b[l]))
        x = out.reshape(B, S, D)
    return x
```

```python
import numpy as np

import jax
import jax.numpy as jnp
from jax import lax
from jax.experimental import pallas as pl
from jax.experimental.pallas import tpu as pltpu

D_MODEL = 1024
ATTN_WIDTH = 512
N_HEADS = 8
HEAD_DIM = 64
LRU_WIDTH = 512
N_LRU_BLOCKS = 8
LRU_BLOCK = 64
CONV_WIDTH = 4
LRU_C = 8.0
MOBA_BLOCK = 256
MOBA_TOPK = 3
D_FF = 4096
LN_EPS = 1e-5
DEEPNORM_ALPHA = 2.0 ** 0.25
NEG_INF = -1e30

SUBLANES = 8
LANES = 128
HEADS_PER_LANE_TILE = LANES // HEAD_DIM
N_HEAD_PAIRS = N_HEADS // HEADS_PER_LANE_TILE
AUG_WIDTH = N_HEADS * LANES
SEL_LANE0 = HEAD_DIM
ALIBI_PIECES = 3
VT_ROWS = HEAD_DIM + 16
LOG2E = 1.4426950408889634

PROJ_ROWS = 512
MLP_ROWS = 512
MLP_SUB_ROWS = 256
MLP_FF_TILE = 512
VMEM_LIMIT_BYTES = 56 * 1024 * 1024

_F32 = jnp.float32
_BF16 = jnp.bfloat16
_NT_DIMS = (((1,), (1,)), ((), ()))


def _proj_kernel(x_ref, w_ref, qx_ref, kx_ref, cw_ref, cb_ref, wrg_ref, brg_ref, wig_ref,
                 big_ref, lam_ref, q_ref, k_ref, vt_ref, km_ref, lru_ref, xbuf, hcar):
    t = pl.program_id(1)
    R = PROJ_ROWS
    C = LRU_WIDTH
    PAD = SUBLANES

    @pl.when(t == 0)
    def _():
        xbuf[0:PAD, :] = jnp.zeros((PAD, C), _F32)
        hcar[...] = jnp.zeros_like(hcar)

    xb = x_ref[0].astype(_BF16)

    def proj(c0, width):
        return jnp.dot(xb, w_ref[:, c0:c0 + width], preferred_element_type=_F32)

    data_lane = lax.broadcasted_iota(jnp.int32, (R, LANES), 1) < HEAD_DIM

    def head_tiles(p512):
        tiles = []
        for pair in range(N_HEAD_PAIRS):
            two = p512[:, pair * LANES:(pair + 1) * LANES]
            tiles += [two, pltpu.roll(two, HEAD_DIM, 1)]
        return tiles

    p_lru = proj(3 * ATTN_WIDTH, 2 * C)
    xr = p_lru[:, :C]
    gr = p_lru[:, C:]
    p_q = proj(0, ATTN_WIDTH)

    xbuf[PAD:PAD + R, :] = xr
    y = cb_ref[...] + cw_ref[CONV_WIDTH - 1:CONV_WIDTH, :] * xr
    for d in range(1, CONV_WIDTH):
        y = y + cw_ref[CONV_WIDTH - 1 - d:CONV_WIDTH - d, :] * xbuf[PAD - d:PAD - d + R, :]
    xbuf[0:PAD, :] = xr[R - PAD:R, :]

    yb = y.astype(_BF16)
    half = C // 2

    def gate_lin(w3_ref, b_ref):
        parts = [jnp.dot(yb[:, s * half:(s + 1) * half], w3_ref[s], preferred_element_type=_F32)
                 for s in range(2)]
        return jnp.concatenate(parts, axis=1) + b_ref[...]

    def sigmoid(v):
        return 0.5 * jnp.tanh(0.5 * v) + 0.5

    rg_pre = gate_lin(wrg_ref, brg_ref)
    ig_pre = gate_lin(wig_ref, big_ref)
    p_k = proj(ATTN_WIDTH, ATTN_WIDTH)

    r = sigmoid(rg_pre)
    ig = sigmoid(ig_pre)
    lam = lam_ref[...]
    softplus_neg_lam = jnp.maximum(-lam, 0.0) + jnp.log(1.0 + jnp.exp(-jnp.abs(lam)))
    a = jnp.exp(r * ((-LRU_C) * softplus_neg_lam))
    v1 = 1.0 - a * a
    u = jnp.where(v1 > 0.0, v1 * lax.rsqrt(v1), 0.0) * (ig * y)
    p_v = proj(2 * ATTN_WIDTH, ATTN_WIDTH)

    qx = qx_ref[...]
    for h, qh in enumerate(head_tiles(p_q * (HEAD_DIM ** -0.5 * LOG2E))):
        q_ref[0, :, h * LANES:(h + 1) * LANES] = jnp.where(data_lane, qh, qx).astype(_BF16)

    row = lax.broadcasted_iota(jnp.int32, (SUBLANES, C), 0)
    carry = hcar[...]
    h_groups = []
    for g in range(R // SUBLANES):
        ag = a[g * SUBLANES:(g + 1) * SUBLANES, :]
        ug = u[g * SUBLANES:(g + 1) * SUBLANES, :]
        for d in (1, 2, 4):
            a_sh = pltpu.roll(ag, d, 0)
            u_sh = pltpu.roll(ug, d, 0)
            m = row >= d
            ug = jnp.where(m, ag * u_sh + ug, ug)
            ag = jnp.where(m, ag * a_sh, ag)
        hg = ag * carry + ug
        h_groups.append(hg)
        carry = jnp.broadcast_to(hg[SUBLANES - 1:SUBLANES, :], (SUBLANES, C))
    hcar[...] = carry

    for h, kh in enumerate(head_tiles(p_k)):
        cols = slice(h * LANES, (h + 1) * LANES)
        k_ref[0, :, cols] = jnp.where(data_lane, kh.astype(_BF16), kx_ref[:, cols])
        kpad = jnp.where(data_lane, kh, 0.0)
        for s in range(R // MOBA_BLOCK):
            km_ref[0, s, :, cols] = jnp.mean(kpad[s * MOBA_BLOCK:(s + 1) * MOBA_BLOCK],
                                             axis=0, keepdims=True)

    vt = p_v.T.astype(_BF16)
    ones_rows = (lax.broadcasted_iota(jnp.int32, (VT_ROWS - HEAD_DIM, R), 0) == 0).astype(_BF16)
    for h in range(N_HEADS):
        vt_ref[0, h * VT_ROWS:h * VT_ROWS + HEAD_DIM, :] = vt[h * HEAD_DIM:(h + 1) * HEAD_DIM, :]
        vt_ref[0, h * VT_ROWS + HEAD_DIM:(h + 1) * VT_ROWS, :] = ones_rows

    gelu = 0.5 * gr * (1.0 + jnp.tanh(0.7978845608028654 * (gr + 0.044715 * (gr * gr * gr))))
    lru_ref[0] = (jnp.concatenate(h_groups, axis=0) * gelu).astype(lru_ref.dtype)


def _proj_lru(x, w_in, qx, kx, conv_w, conv_b, wrg, b_rg, wig, b_ig, lam):
    B, S, D = x.shape
    C = LRU_WIDTH
    nb = S // MOBA_BLOCK
    bps = PROJ_ROWS // MOBA_BLOCK
    resident = lambda shape: pl.BlockSpec(shape, lambda b, t: (0,) * len(shape),
                                          pipeline_mode=pl.Buffered(1))
    return pl.pallas_call(
        _proj_kernel,
        grid=(B, S // PROJ_ROWS),
        in_specs=[
            pl.BlockSpec((1, PROJ_ROWS, D), lambda b, t: (b, t, 0)),
            resident((D, 3 * ATTN_WIDTH + 2 * C)),
            resident((1, LANES)),
            pl.BlockSpec((PROJ_ROWS, AUG_WIDTH), lambda b, t: (t, 0)),
            resident((CONV_WIDTH, C)),
            resident((1, C)),
            resident((2, C // 2, C // 2)),
            resident((1, C)),
            resident((2, C // 2, C // 2)),
            resident((1, C)),
            resident((1, C)),
        ],
        out_specs=[
            pl.BlockSpec((1, PROJ_ROWS, AUG_WIDTH), lambda b, t: (b, t, 0)),
            pl.BlockSpec((1, PROJ_ROWS, AUG_WIDTH), lambda b, t: (b, t, 0)),
            pl.BlockSpec((1, N_HEADS * VT_ROWS, PROJ_ROWS), lambda b, t: (b, 0, t)),
            pl.BlockSpec((1, bps, 1, AUG_WIDTH), lambda b, t: (b, t, 0, 0)),
            pl.BlockSpec((1, PROJ_ROWS, C), lambda b, t: (b, t, 0)),
        ],
        out_shape=[
            jax.ShapeDtypeStruct((B, S, AUG_WIDTH), _BF16),
            jax.ShapeDtypeStruct((B, S, AUG_WIDTH), _BF16),
            jax.ShapeDtypeStruct((B, N_HEADS * VT_ROWS, S), _BF16),
            jax.ShapeDtypeStruct((B, nb, 1, AUG_WIDTH), _F32),
            jax.ShapeDtypeStruct((B, S, C), _BF16),
        ],
        scratch_shapes=[
            pltpu.VMEM((PROJ_ROWS + SUBLANES, C), _F32),
            pltpu.VMEM((SUBLANES, C), _F32),
        ],
        compiler_params=pltpu.CompilerParams(
            dimension_semantics=("parallel", "arbitrary"),
            vmem_limit_bytes=VMEM_LIMIT_BYTES),
        name="proj_lru",
    )(x, w_in, qx, kx, conv_w, conv_b, wrg, b_rg, wig, b_ig, lam)


def _aug_constants(seq_len):
    nb = seq_len // MOBA_BLOCK
    alibi_lane0 = SEL_LANE0 + nb
    lane = np.arange(LANES)
    qx = ((lane >= alibi_lane0) & (lane < alibi_lane0 + ALIBI_PIECES)).astype(np.float32).reshape(1, LANES)

    slopes = np.exp2(-8.0 * np.arange(1, N_HEADS + 1, dtype=np.float32) / N_HEADS).astype(np.float32)
    pos = np.arange(seq_len)
    onehot = (pos[:, None] // MOBA_BLOCK == np.arange(nb)[None, :]).astype(np.float32)
    val = (np.float32(LOG2E) * slopes)[None, :] * (pos[:, None] - seq_len // 2).astype(np.float32)

    def trunc_bf16(v):
        return (v.view(np.uint32) & np.uint32(0xFFFF0000)).view(np.float32)

    hi = trunc_bf16(val)
    mid = trunc_bf16(val - hi)
    lo = val - hi - mid
    pieces = np.stack([hi, mid, lo], axis=-1)
    extras = np.concatenate([
        np.zeros((seq_len, N_HEADS, HEAD_DIM), np.float32),
        np.broadcast_to(onehot[:, None, :], (seq_len, N_HEADS, nb)),
        pieces,
        np.zeros((seq_len, N_HEADS, LANES - alibi_lane0 - ALIBI_PIECES), np.float32)], axis=-1)
    return jnp.asarray(qx), jnp.asarray(extras.reshape(seq_len, AUG_WIDTH).astype(_BF16))


def _attn_kernel(q_ref, k_ref, vt_ref, km_ref, o_ref):
    blk = MOBA_BLOCK
    nb = k_ref.shape[1] // blk

    lane = lax.broadcasted_iota(jnp.int32, (blk, LANES), 1)
    sel_lane = (lane >= SEL_LANE0) & (lane < SEL_LANE0 + nb)
    key_row = lax.broadcasted_iota(jnp.int32, (blk, blk), 0)
    qry_col = lax.broadcasted_iota(jnp.int32, (blk, blk), 1)
    causal = key_row <= qry_col
    blk_idx = lax.broadcasted_iota(jnp.int32, (nb, blk), 0)

    def scores(j, hh):
        nk = (j + 1) * blk
        cols = slice(hh * LANES, (hh + 1) * LANES)
        qa = q_ref[0, j * blk:nk, cols]
        if j > MOBA_TOPK:
            kmh = km_ref[0, :, 0, cols].astype(_BF16)
            gate = lax.dot_general(kmh, qa, _NT_DIMS, preferred_element_type=_F32)
            rank = jnp.zeros((nb, blk), jnp.int32)
            for n2 in range(j):
                g2 = gate[n2:n2 + 1, :]
                beats = (g2 > gate) | ((g2 == gate) & (n2 < blk_idx))
                rank = rank + beats.astype(jnp.int32)
            keep = ((blk_idx < j) & (rank < MOBA_TOPK)) | (blk_idx >= j)
            selb = jnp.where(keep, 0.0, NEG_INF)
            selb_rows = jnp.concatenate(
                [jnp.zeros((SEL_LANE0, blk), _F32), selb,
                 jnp.zeros((LANES - SEL_LANE0 - nb, blk), _F32)], axis=0)
            qa = jnp.where(sel_lane, selb_rows.T.astype(_BF16), qa)
        z = lax.dot_general(k_ref[0, 0:nk, cols], qa, _NT_DIMS,
                            preferred_element_type=_F32)
        z_own = jnp.where(causal, z[nk - blk:, :], NEG_INF)
        z = z_own if j == 0 else jnp.concatenate([z[:nk - blk, :], z_own], axis=0)
        m = jnp.max(z, axis=0, keepdims=True)
        return z, m

    def probs(zm):
        z, m = zm
        return jnp.exp2(z - m).astype(_BF16)

    def weighted_values(j, hh, p):
        nk = (j + 1) * blk
        vt = vt_ref[0, hh * VT_ROWS:(hh + 1) * VT_ROWS, 0:nk]
        acc = jnp.dot(vt, p, preferred_element_type=_F32)
        return acc[0:HEAD_DIM, :] / acc[HEAD_DIM:HEAD_DIM + 1, :]

    items = [(j, hh) for j in range(nb) for hh in range(HEADS_PER_LANE_TILE)]
    n_items = len(items)
    zm = {0: scores(*items[0])}
    pls = {}
    outs = {}
    for t in range(n_items + 1):
        if t + 1 < n_items:
            zm[t + 1] = scores(*items[t + 1])
        if t < n_items:
            pls[t] = probs(zm.pop(t))
        if t >= 1:
            j, hh = items[t - 1]
            outs[hh] = weighted_values(j, hh, pls.pop(t - 1))
            if hh == HEADS_PER_LANE_TILE - 1:
                o_ref[0, j * blk:(j + 1) * blk, :] = jnp.concatenate(
                    [outs[h] for h in range(HEADS_PER_LANE_TILE)], axis=0).T.astype(o_ref.dtype)


def _moba_attention(q, k, vt, km):
    B, S, _ = q.shape
    nb = S // MOBA_BLOCK
    pair_w = HEADS_PER_LANE_TILE * LANES
    return pl.pallas_call(
        _attn_kernel,
        grid=(B, N_HEAD_PAIRS),
        in_specs=[
            pl.BlockSpec((1, S, pair_w), lambda b, h: (b, 0, h)),
            pl.BlockSpec((1, S, pair_w), lambda b, h: (b, 0, h)),
            pl.BlockSpec((1, HEADS_PER_LANE_TILE * VT_ROWS, S), lambda b, h: (b, h, 0)),
            pl.BlockSpec((1, nb, 1, pair_w), lambda b, h: (b, 0, 0, h)),
        ],
        out_specs=pl.BlockSpec((1, S, LANES), lambda b, h: (b, 0, h)),
        out_shape=jax.ShapeDtypeStruct((B, S, ATTN_WIDTH), _BF16),
        compiler_params=pltpu.CompilerParams(
            dimension_semantics=("parallel", "parallel"),
            vmem_limit_bytes=VMEM_LIMIT_BYTES),
        name="moba_attention",
    )(q, k, vt, km)


def _layer_norm(y, g, b):
    mu = jnp.mean(y, axis=-1, keepdims=True)
    yc = y - mu
    var = jnp.mean(yc * yc, axis=-1, keepdims=True)
    return yc * lax.rsqrt(var + LN_EPS) * g + b


def _mlp_kernel(x_ref, attn_ref, lru_ref, wo_ref, g1_ref, b1_ref, wup_ref, bup_ref,
                wdn_ref, bdn_ref, g2_ref, b2_ref, o_ref):
    n_sub = MLP_ROWS // MLP_SUB_ROWS
    n_ff = wup_ref.shape[1] // MLP_FF_TILE

    def rows(c):
        return slice(c * MLP_SUB_ROWS, (c + 1) * MLP_SUB_ROWS)

    def out_proj(c):
        mix = jnp.dot(attn_ref[rows(c), :], wo_ref[0:ATTN_WIDTH, :], preferred_element_type=_F32)
        return mix + jnp.dot(lru_ref[rows(c), :], wo_ref[ATTN_WIDTH:, :], preferred_element_type=_F32)

    def norm1(c, mix):
        x1 = _layer_norm(DEEPNORM_ALPHA * x_ref[rows(c), :] + mix, g1_ref[...], b1_ref[...])
        return x1, x1.astype(_BF16)

    def ff_tile(x1b, f):
        cols = slice(f * MLP_FF_TILE, (f + 1) * MLP_FF_TILE)
        h = jnp.dot(x1b, wup_ref[:, cols], preferred_element_type=_F32) + bup_ref[:, cols]
        h = jnp.maximum(h, 0.0)
        return jnp.dot((h * h).astype(_BF16), wdn_ref[cols, :], preferred_element_type=_F32)

    def norm2(c, x1, acc):
        y = DEEPNORM_ALPHA * x1 + acc + bdn_ref[...]
        o_ref[rows(c), :] = _layer_norm(y, g2_ref[...], b2_ref[...])

    x1, x1b = norm1(0, out_proj(0))
    pending = None
    for c in range(n_sub):
        nxt_mix = out_proj(c + 1) if c + 1 < n_sub else None
        nxt = None
        acc = None
        for f in range(n_ff):
            part = ff_tile(x1b, f)
            acc = part if acc is None else acc + part
            if f == 1 and nxt_mix is not None:
                nxt = norm1(c + 1, nxt_mix)
            if f == n_ff // 2 and pending is not None:
                norm2(*pending)
                pending = None
        pending = (c, x1, acc)
        if nxt is not None:
            x1, x1b = nxt
    norm2(*pending)


def _outproj_mlp(x2, attn2, lru2, w_out, g1, b1, w_up, b_up, w_down, b_down, g2, b2):
    M, D = x2.shape
    F = w_up.shape[1]
    tm = MLP_ROWS
    resident = lambda shape: pl.BlockSpec(shape, lambda i: (0, 0), pipeline_mode=pl.Buffered(1))
    return pl.pallas_call(
        _mlp_kernel,
        grid=(M // tm,),
        in_specs=[
            pl.BlockSpec((tm, D), lambda i: (i, 0)),
            pl.BlockSpec((tm, ATTN_WIDTH), lambda i: (i, 0)),
            pl.BlockSpec((tm, LRU_WIDTH), lambda i: (i, 0)),
            resident((D, D)),
            resident((1, D)), resident((1, D)),
            resident((D, F)),
            resident((1, F)),
            resident((F, D)),
            resident((1, D)), resident((1, D)), resident((1, D)),
        ],
        out_specs=pl.BlockSpec((tm, D), lambda i: (i, 0)),
        out_shape=jax.ShapeDtypeStruct((M, D), _F32),
        compiler_params=pltpu.CompilerParams(
            dimension_semantics=("parallel",),
            vmem_limit_bytes=VMEM_LIMIT_BYTES),
        name="outproj_mlp",
    )(x2, attn2, lru2, w_out, g1, b1, w_up, b_up, w_down, b_down, g2, b2)


def _block_diag_halves(w):
    n_half = N_LRU_BLOCKS // 2
    eye = jnp.eye(n_half, dtype=w.dtype)
    wh = w.reshape(2, n_half, LRU_BLOCK, LRU_BLOCK)
    full = jnp.einsum('snde,nm->sndme', wh, eye)
    return full.reshape(2, n_half * LRU_BLOCK, n_half * LRU_BLOCK)


def kernel(x, w_in, conv_w, conv_b, w_rg, b_rg, w_ig, b_ig, lru_lambda, w_out, ln1_g, ln1_b,
           w_up, b_up, w_down, b_down, ln2_g, ln2_b):
    B, S, D = x.shape
    depth = w_in.shape[0]
    row = lambda v: v.reshape(1, -1)
    qx, kx = _aug_constants(S)
    for l in range(depth):
        q, k, vt, km, lru = _proj_lru(
            x, w_in[l].astype(_BF16), qx, kx, conv_w[l], row(conv_b[l]),
            _block_diag_halves(w_rg[l]).astype(_BF16), row(b_rg[l]),
            _block_diag_halves(w_ig[l]).astype(_BF16), row(b_ig[l]),
            row(lru_lambda[l]))
        attn = _moba_attention(q, k, vt, km)
        out = _outproj_mlp(
            x.reshape(B * S, D), attn.reshape(B * S, ATTN_WIDTH), lru.reshape(B * S, LRU_WIDTH),
            w_out[l].astype(_BF16), row(ln1_g[l]), row(ln1_b[l]),
            w_up[l].astype(_BF16), row(b_up[l]), w_down[l].astype(_BF16), row(b_down[l]),
            row(ln2_g[l]), row(ln2_b[l]))
        x = out.reshape(B, S, D)
    return x
```

```python
import numpy as np

import jax
import jax.numpy as jnp
from jax import lax
from jax.experimental import pallas as pl
from jax.experimental.pallas import tpu as pltpu

D_MODEL = 1024
ATTN_WIDTH = 512
N_HEADS = 8
HEAD_DIM = 64
LRU_WIDTH = 512
N_LRU_BLOCKS = 8
LRU_BLOCK = 64
CONV_WIDTH = 4
LRU_C = 8.0
MOBA_BLOCK = 256
MOBA_TOPK = 3
D_FF = 4096
LN_EPS = 1e-5
DEEPNORM_ALPHA = 2.0 ** 0.25
NEG_INF = -1e30

SUBLANES = 8
LANES = 128
HEADS_PER_LANE_TILE = LANES // HEAD_DIM
N_HEAD_PAIRS = N_HEADS // HEADS_PER_LANE_TILE
AUG_WIDTH = N_HEADS * LANES
SEL_LANE0 = HEAD_DIM
ALIBI_PIECES = 3
VT_ROWS = HEAD_DIM + 16
LOG2E = 1.4426950408889634

ATTN_STAGE_LAG = 2
PROJ_ROWS = 512
MLP_ROWS = 512
MLP_SUB_ROWS = 256
MLP_FF_TILE = 512
VMEM_LIMIT_BYTES = 56 * 1024 * 1024

_F32 = jnp.float32
_BF16 = jnp.bfloat16
_NT_DIMS = (((1,), (1,)), ((), ()))


def _proj_kernel(x_ref, w_ref, qx_ref, kx_ref, cw_ref, cb_ref, wrg_ref, brg_ref, wig_ref,
                 big_ref, lam_ref, q_ref, k_ref, vt_ref, km_ref, lru_ref, xbuf, hcar):
    t = pl.program_id(1)
    R = PROJ_ROWS
    C = LRU_WIDTH
    PAD = SUBLANES

    @pl.when(t == 0)
    def _():
        xbuf[0:PAD, :] = jnp.zeros((PAD, C), _F32)
        hcar[...] = jnp.zeros_like(hcar)

    xb = x_ref[0].astype(_BF16)

    def proj(c0, width):
        return jnp.dot(xb, w_ref[:, c0:c0 + width], preferred_element_type=_F32)

    data_lane = lax.broadcasted_iota(jnp.int32, (R, LANES), 1) < HEAD_DIM

    def head_tiles(p512):
        tiles = []
        for pair in range(N_HEAD_PAIRS):
            two = p512[:, pair * LANES:(pair + 1) * LANES]
            tiles += [two, pltpu.roll(two, HEAD_DIM, 1)]
        return tiles

    p_lru = proj(3 * ATTN_WIDTH, 2 * C)
    xr = p_lru[:, :C]
    gr = p_lru[:, C:]
    p_q = proj(0, ATTN_WIDTH)

    xbuf[PAD:PAD + R, :] = xr
    y = cb_ref[...] + cw_ref[CONV_WIDTH - 1:CONV_WIDTH, :] * xr
    for d in range(1, CONV_WIDTH):
        y = y + cw_ref[CONV_WIDTH - 1 - d:CONV_WIDTH - d, :] * xbuf[PAD - d:PAD - d + R, :]
    xbuf[0:PAD, :] = xr[R - PAD:R, :]

    yb = y.astype(_BF16)
    half = C // 2

    def gate_lin(w3_ref, b_ref):
        parts = [jnp.dot(yb[:, s * half:(s + 1) * half], w3_ref[s], preferred_element_type=_F32)
                 for s in range(2)]
        return jnp.concatenate(parts, axis=1) + b_ref[...]

    def sigmoid(v):
        return 0.5 * jnp.tanh(0.5 * v) + 0.5

    rg_pre = gate_lin(wrg_ref, brg_ref)
    ig_pre = gate_lin(wig_ref, big_ref)
    p_k = proj(ATTN_WIDTH, ATTN_WIDTH)

    r = sigmoid(rg_pre)
    ig = sigmoid(ig_pre)
    lam = lam_ref[...]
    softplus_neg_lam = jnp.maximum(-lam, 0.0) + jnp.log(1.0 + jnp.exp(-jnp.abs(lam)))
    a = jnp.exp(r * ((-LRU_C) * softplus_neg_lam))
    v1 = 1.0 - a * a
    u = jnp.where(v1 > 0.0, v1 * lax.rsqrt(v1), 0.0) * (ig * y)
    p_v = proj(2 * ATTN_WIDTH, ATTN_WIDTH)

    qx = qx_ref[...]
    for h, qh in enumerate(head_tiles(p_q * (HEAD_DIM ** -0.5 * LOG2E))):
        q_ref[0, :, h * LANES:(h + 1) * LANES] = jnp.where(data_lane, qh, qx).astype(_BF16)

    row = lax.broadcasted_iota(jnp.int32, (SUBLANES, C), 0)
    carry = hcar[...]
    h_groups = []
    for g in range(R // SUBLANES):
        ag = a[g * SUBLANES:(g + 1) * SUBLANES, :]
        ug = u[g * SUBLANES:(g + 1) * SUBLANES, :]
        for d in (1, 2, 4):
            a_sh = pltpu.roll(ag, d, 0)
            u_sh = pltpu.roll(ug, d, 0)
            m = row >= d
            ug = jnp.where(m, ag * u_sh + ug, ug)
            ag = jnp.where(m, ag * a_sh, ag)
        hg = ag * carry + ug
        h_groups.append(hg)
        carry = jnp.broadcast_to(hg[SUBLANES - 1:SUBLANES, :], (SUBLANES, C))
    hcar[...] = carry

    for h, kh in enumerate(head_tiles(p_k)):
        cols = slice(h * LANES, (h + 1) * LANES)
        k_ref[0, :, cols] = jnp.where(data_lane, kh.astype(_BF16), kx_ref[:, cols])
        kpad = jnp.where(data_lane, kh, 0.0)
        for s in range(R // MOBA_BLOCK):
            km_ref[0, s, :, cols] = jnp.mean(kpad[s * MOBA_BLOCK:(s + 1) * MOBA_BLOCK],
                                             axis=0, keepdims=True)

    vt = p_v.T.astype(_BF16)
    ones_rows = (lax.broadcasted_iota(jnp.int32, (VT_ROWS - HEAD_DIM, R), 0) == 0).astype(_BF16)
    for h in range(N_HEADS):
        vt_ref[0, h * VT_ROWS:h * VT_ROWS + HEAD_DIM, :] = vt[h * HEAD_DIM:(h + 1) * HEAD_DIM, :]
        vt_ref[0, h * VT_ROWS + HEAD_DIM:(h + 1) * VT_ROWS, :] = ones_rows

    gelu = 0.5 * gr * (1.0 + jnp.tanh(0.7978845608028654 * (gr + 0.044715 * (gr * gr * gr))))
    lru_ref[0] = (jnp.concatenate(h_groups, axis=0) * gelu).astype(lru_ref.dtype)


def _proj_lru(x, w_in, qx, kx, conv_w, conv_b, wrg, b_rg, wig, b_ig, lam):
    B, S, D = x.shape
    C = LRU_WIDTH
    nb = S // MOBA_BLOCK
    bps = PROJ_ROWS // MOBA_BLOCK
    resident = lambda shape: pl.BlockSpec(shape, lambda b, t: (0,) * len(shape),
                                          pipeline_mode=pl.Buffered(1))
    return pl.pallas_call(
        _proj_kernel,
        grid=(B, S // PROJ_ROWS),
        in_specs=[
            pl.BlockSpec((1, PROJ_ROWS, D), lambda b, t: (b, t, 0)),
            resident((D, 3 * ATTN_WIDTH + 2 * C)),
            resident((1, LANES)),
            pl.BlockSpec((PROJ_ROWS, AUG_WIDTH), lambda b, t: (t, 0)),
            resident((CONV_WIDTH, C)),
            resident((1, C)),
            resident((2, C // 2, C // 2)),
            resident((1, C)),
            resident((2, C // 2, C // 2)),
            resident((1, C)),
            resident((1, C)),
        ],
        out_specs=[
            pl.BlockSpec((1, PROJ_ROWS, AUG_WIDTH), lambda b, t: (b, t, 0)),
            pl.BlockSpec((1, PROJ_ROWS, AUG_WIDTH), lambda b, t: (b, t, 0)),
            pl.BlockSpec((1, N_HEADS * VT_ROWS, PROJ_ROWS), lambda b, t: (b, 0, t)),
            pl.BlockSpec((1, bps, 1, AUG_WIDTH), lambda b, t: (b, t, 0, 0)),
            pl.BlockSpec((1, PROJ_ROWS, C), lambda b, t: (b, t, 0)),
        ],
        out_shape=[
            jax.ShapeDtypeStruct((B, S, AUG_WIDTH), _BF16),
            jax.ShapeDtypeStruct((B, S, AUG_WIDTH), _BF16),
            jax.ShapeDtypeStruct((B, N_HEADS * VT_ROWS, S), _BF16),
            jax.ShapeDtypeStruct((B, nb, 1, AUG_WIDTH), _F32),
            jax.ShapeDtypeStruct((B, S, C), _BF16),
        ],
        scratch_shapes=[
            pltpu.VMEM((PROJ_ROWS + SUBLANES, C), _F32),
            pltpu.VMEM((SUBLANES, C), _F32),
        ],
        compiler_params=pltpu.CompilerParams(
            dimension_semantics=("parallel", "arbitrary"),
            vmem_limit_bytes=VMEM_LIMIT_BYTES),
        name="proj_lru",
    )(x, w_in, qx, kx, conv_w, conv_b, wrg, b_rg, wig, b_ig, lam)


def _aug_constants(seq_len):
    nb = seq_len // MOBA_BLOCK
    alibi_lane0 = SEL_LANE0 + nb
    lane = np.arange(LANES)
    qx = ((lane >= alibi_lane0) & (lane < alibi_lane0 + ALIBI_PIECES)).astype(np.float32).reshape(1, LANES)

    slopes = np.exp2(-8.0 * np.arange(1, N_HEADS + 1, dtype=np.float32) / N_HEADS).astype(np.float32)
    pos = np.arange(seq_len)
    onehot = (pos[:, None] // MOBA_BLOCK == np.arange(nb)[None, :]).astype(np.float32)
    val = (np.float32(LOG2E) * slopes)[None, :] * (pos[:, None] - seq_len // 2).astype(np.float32)

    def trunc_bf16(v):
        return (v.view(np.uint32) & np.uint32(0xFFFF0000)).view(np.float32)

    hi = trunc_bf16(val)
    mid = trunc_bf16(val - hi)
    lo = val - hi - mid
    pieces = np.stack([hi, mid, lo], axis=-1)
    extras = np.concatenate([
        np.zeros((seq_len, N_HEADS, HEAD_DIM), np.float32),
        np.broadcast_to(onehot[:, None, :], (seq_len, N_HEADS, nb)),
        pieces,
        np.zeros((seq_len, N_HEADS, LANES - alibi_lane0 - ALIBI_PIECES), np.float32)], axis=-1)
    return jnp.asarray(qx), jnp.asarray(extras.reshape(seq_len, AUG_WIDTH).astype(_BF16))


def _attn_kernel(q_ref, k_ref, vt_ref, km_ref, o_ref):
    blk = MOBA_BLOCK
    nb = k_ref.shape[1] // blk

    lane = lax.broadcasted_iota(jnp.int32, (blk, LANES), 1)
    sel_lane = (lane >= SEL_LANE0) & (lane < SEL_LANE0 + nb)
    key_row = lax.broadcasted_iota(jnp.int32, (blk, blk), 0)
    qry_col = lax.broadcasted_iota(jnp.int32, (blk, blk), 1)
    causal = key_row <= qry_col
    blk_idx = lax.broadcasted_iota(jnp.int32, (nb, blk), 0)

    def query_tile(j, hh):
        nk = (j + 1) * blk
        cols = slice(hh * LANES, (hh + 1) * LANES)
        qa = q_ref[0, j * blk:nk, cols]
        if j > MOBA_TOPK:
            kmh = km_ref[0, :, 0, cols].astype(_BF16)
            gate = lax.dot_general(kmh, qa, _NT_DIMS, preferred_element_type=_F32)
            rank = jnp.zeros((nb, blk), jnp.int32)
            for n2 in range(j):
                g2 = gate[n2:n2 + 1, :]
                beats = (g2 > gate) | ((g2 == gate) & (n2 < blk_idx))
                rank = rank + beats.astype(jnp.int32)
            keep = ((blk_idx < j) & (rank < MOBA_TOPK)) | (blk_idx >= j)
            selb = jnp.where(keep, 0.0, NEG_INF)
            selb_rows = jnp.concatenate(
                [jnp.zeros((SEL_LANE0, blk), _F32), selb,
                 jnp.zeros((LANES - SEL_LANE0 - nb, blk), _F32)], axis=0)
            qa = jnp.where(sel_lane, selb_rows.T.astype(_BF16), qa)
        return qa

    def scores(j, hh):
        nk = (j + 1) * blk
        cols = slice(hh * LANES, (hh + 1) * LANES)
        z = lax.dot_general(k_ref[0, 0:nk, cols], query_tile(j, hh), _NT_DIMS,
                            preferred_element_type=_F32)
        z_own = jnp.where(causal, z[nk - blk:, :], NEG_INF)
        z = z_own if j == 0 else jnp.concatenate([z[:nk - blk, :], z_own], axis=0)
        m = jnp.max(z, axis=0, keepdims=True)
        return z, m

    def probs(zm):
        z, m = zm
        return jnp.exp2(z - m).astype(_BF16)

    def weighted_values(j, hh, p):
        nk = (j + 1) * blk
        vt = vt_ref[0, hh * VT_ROWS:(hh + 1) * VT_ROWS, 0:nk]
        acc = jnp.dot(vt, p, preferred_element_type=_F32)
        return acc[0:HEAD_DIM, :] / acc[HEAD_DIM:HEAD_DIM + 1, :]

    items = [(j, hh) for j in range(nb) for hh in range(HEADS_PER_LANE_TILE)]
    n_items = len(items)
    zm, ps, outs = {}, {}, {}
    for t in range(-2 * ATTN_STAGE_LAG, n_items):
        if t + 2 * ATTN_STAGE_LAG < n_items:
            zm[t + 2 * ATTN_STAGE_LAG] = scores(*items[t + 2 * ATTN_STAGE_LAG])
        if 0 <= t + ATTN_STAGE_LAG < n_items:
            ps[t + ATTN_STAGE_LAG] = probs(zm.pop(t + ATTN_STAGE_LAG))
        if t >= 0:
            j, hh = items[t]
            outs[hh] = weighted_values(j, hh, ps.pop(t))
            if hh == HEADS_PER_LANE_TILE - 1:
                o_ref[0, j * blk:(j + 1) * blk, :] = jnp.concatenate(
                    [outs[h] for h in range(HEADS_PER_LANE_TILE)], axis=0).T.astype(o_ref.dtype)


def _moba_attention(q, k, vt, km):
    B, S, _ = q.shape
    nb = S // MOBA_BLOCK
    pair_w = HEADS_PER_LANE_TILE * LANES
    return pl.pallas_call(
        _attn_kernel,
        grid=(B, N_HEAD_PAIRS),
        in_specs=[
            pl.BlockSpec((1, S, pair_w), lambda b, h: (b, 0, h)),
            pl.BlockSpec((1, S, pair_w), lambda b, h: (b, 0, h)),
            pl.BlockSpec((1, HEADS_PER_LANE_TILE * VT_ROWS, S), lambda b, h: (b, h, 0)),
            pl.BlockSpec((1, nb, 1, pair_w), lambda b, h: (b, 0, 0, h)),
        ],
        out_specs=pl.BlockSpec((1, S, LANES), lambda b, h: (b, 0, h)),
        out_shape=jax.ShapeDtypeStruct((B, S, ATTN_WIDTH), _BF16),
        compiler_params=pltpu.CompilerParams(
            dimension_semantics=("parallel", "parallel"),
            vmem_limit_bytes=VMEM_LIMIT_BYTES),
        name="moba_attention",
    )(q, k, vt, km)


def _layer_norm(y, g, b):
    mu = jnp.mean(y, axis=-1, keepdims=True)
    yc = y - mu
    var = jnp.mean(yc * yc, axis=-1, keepdims=True)
    return yc * lax.rsqrt(var + LN_EPS) * g + b


def _mlp_kernel(x_ref, attn_ref, lru_ref, wo_ref, g1_ref, b1_ref, wup_ref, bup_ref,
                wdn_ref, bdn_ref, g2_ref, b2_ref, o_ref):
    n_sub = MLP_ROWS // MLP_SUB_ROWS
    n_ff = wup_ref.shape[1] // MLP_FF_TILE

    def rows(c):
        return slice(c * MLP_SUB_ROWS, (c + 1) * MLP_SUB_ROWS)

    def out_proj(c):
        mix = jnp.dot(attn_ref[rows(c), :], wo_ref[0:ATTN_WIDTH, :], preferred_element_type=_F32)
        return mix + jnp.dot(lru_ref[rows(c), :], wo_ref[ATTN_WIDTH:, :], preferred_element_type=_F32)

    def norm1(c, mix):
        x1 = _layer_norm(DEEPNORM_ALPHA * x_ref[rows(c), :] + mix, g1_ref[...], b1_ref[...])
        return x1, x1.astype(_BF16)

    def ff_tile(x1b, f):
        cols = slice(f * MLP_FF_TILE, (f + 1) * MLP_FF_TILE)
        h = jnp.dot(x1b, wup_ref[:, cols], preferred_element_type=_F32) + bup_ref[:, cols]
        h = jnp.maximum(h, 0.0)
        return jnp.dot((h * h).astype(_BF16), wdn_ref[cols, :], preferred_element_type=_F32)

    def norm2(c, x1, acc):
        y = DEEPNORM_ALPHA * x1 + acc + bdn_ref[...]
        o_ref[rows(c), :] = _layer_norm(y, g2_ref[...], b2_ref[...])

    x1, x1b = norm1(0, out_proj(0))
    pending = None
    for c in range(n_sub):
        nxt_mix = out_proj(c + 1) if c + 1 < n_sub else None
        nxt = None
        acc = None
        for f in range(n_ff):
            part = ff_tile(x1b, f)
            acc = part if acc is None else acc + part
            if f == 1 and nxt_mix is not None:
                nxt = norm1(c + 1, nxt_mix)
            if f == n_ff // 2 and pending is not None:
                norm2(*pending)
                pending = None
        pending = (c, x1, acc)
        if nxt is not None:
            x1, x1b = nxt
    norm2(*pending)


def _outproj_mlp(x2, attn2, lru2, w_out, g1, b1, w_up, b_up, w_down, b_down, g2, b2):
    M, D = x2.shape
    F = w_up.shape[1]
    tm = MLP_ROWS
    resident = lambda shape: pl.BlockSpec(shape, lambda i: (0, 0), pipeline_mode=pl.Buffered(1))
    return pl.pallas_call(
        _mlp_kernel,
        grid=(M // tm,),
        in_specs=[
            pl.BlockSpec((tm, D), lambda i: (i, 0)),
            pl.BlockSpec((tm, ATTN_WIDTH), lambda i: (i, 0)),
            pl.BlockSpec((tm, LRU_WIDTH), lambda i: (i, 0)),
            resident((D, D)),
            resident((1, D)), resident((1, D)),
            resident((D, F)),
            resident((1, F)),
            resident((F, D)),
            resident((1, D)), resident((1, D)), resident((1, D)),
        ],
        out_specs=pl.BlockSpec((tm, D), lambda i: (i, 0)),
        out_shape=jax.ShapeDtypeStruct((M, D), _F32),
        compiler_params=pltpu.CompilerParams(
            dimension_semantics=("parallel",),
            vmem_limit_bytes=VMEM_LIMIT_BYTES),
        name="outproj_mlp",
    )(x2, attn2, lru2, w_out, g1, b1, w_up, b_up, w_down, b_down, g2, b2)


def _block_diag_halves(w):
    n_half = N_LRU_BLOCKS // 2
    eye = jnp.eye(n_half, dtype=w.dtype)
    wh = w.reshape(2, n_half, LRU_BLOCK, LRU_BLOCK)
    full = jnp.einsum('snde,nm->sndme', wh, eye)
    return full.reshape(2, n_half * LRU_BLOCK, n_half * LRU_BLOCK)


def kernel(x, w_in, conv_w, conv_b, w_rg, b_rg, w_ig, b_ig, lru_lambda, w_out, ln1_g, ln1_b,
           w_up, b_up, w_down, b_down, ln2_g, ln2_b):
    B, S, D = x.shape
    depth = w_in.shape[0]
    row = lambda v: v.reshape(1, -1)
    qx, kx = _aug_constants(S)
    for l in range(depth):
        q, k, vt, km, lru = _proj_lru(
            x, w_in[l].astype(_BF16), qx, kx, conv_w[l], row(conv_b[l]),
            _block_diag_halves(w_rg[l]).astype(_BF16), row(b_rg[l]),
            _block_diag_halves(w_ig[l]).astype(_BF16), row(b_ig[l]),
            row(lru_lambda[l]))
        attn = _moba_attention(q, k, vt, km)
        out = _outproj_mlp(
            x.reshape(B * S, D), attn.reshape(B * S, ATTN_WIDTH), lru.reshape(B * S, LRU_WIDTH),
            w_out[l].astype(_BF16), row(ln1_g[l]), row(ln1_b[l]),
            w_up[l].astype(_BF16), row(b_up[l]), w_down[l].astype(_BF16), row(b_down[l]),
            row(ln2_g[l]), row(ln2_b[l]))
        x = out.reshape(B, S, D)
    return x
```

```python
import functools

import numpy as np

import jax
import jax.numpy as jnp
from jax import lax
from jax.experimental import pallas as pl
from jax.experimental.pallas import tpu as pltpu

D_MODEL = 1024
ATTN_WIDTH = 512
N_HEADS = 8
HEAD_DIM = 64
LRU_WIDTH = 512
N_LRU_BLOCKS = 8
LRU_BLOCK = 64
CONV_WIDTH = 4
LRU_C = 8.0
MOBA_BLOCK = 256
MOBA_TOPK = 3
D_FF = 4096
LN_EPS = 1e-5
DEEPNORM_ALPHA = 2.0 ** 0.25
NEG_INF = -1e30

SUBLANES = 8
LANES = 128
HEADS_PER_LANE_TILE = LANES // HEAD_DIM
N_HEAD_PAIRS = N_HEADS // HEADS_PER_LANE_TILE
AUG_WIDTH = N_HEADS * LANES
SEL_LANE0 = HEAD_DIM
ALIBI_PIECES = 3
VT_ROWS = HEAD_DIM + 16
LOG2E = 1.4426950408889634

ATTN_STAGE_LAG = 2
PROJ_ROWS = 512
PROJ_TILES_PER_STEP = 2
PROJ_COL_CHUNK = 256
PROJ_SUB_ROWS = 128
MLP_ROWS = 512
MLP_SUB_ROWS = 256
MLP_FF_TILE = 512
VMEM_LIMIT_BYTES = 56 * 1024 * 1024

_F32 = jnp.float32
_BF16 = jnp.bfloat16
_NT_DIMS = (((1,), (1,)), ((), ()))


def _proj_kernel(x0_ref, xa_ref, xb_ref, w_ref, qx_ref, kx_ref, cw_ref, cb_ref, wrg_ref, brg_ref,
                 wig_ref, big_ref, lam_ref, q_ref, k_ref, vt_ref, km_ref, lru_ref,
                 p_s0, p_s1, xbuf, hcar, *, steps_per_seq):
    i = pl.program_id(0)
    R = PROJ_ROWS
    C = LRU_WIDTH
    SUB = PROJ_SUB_ROWS
    PAD = SUBLANES
    Q0, K0, V0, XR0, GR0 = 0, ATTN_WIDTH, 2 * ATTN_WIDTH, 3 * ATTN_WIDTH, 3 * ATTN_WIDTH + C

    slots = (p_s0, p_s1)

    def project(x_tile_ref, slot_ref, c0, width):
        xt = x_tile_ref[...].astype(_BF16)
        slot_ref[:, c0:c0 + width] = jnp.dot(xt, w_ref[:, c0:c0 + width],
                                             preferred_element_type=_F32)

    @pl.when(i == 0)
    def _():
        project(x0_ref, p_s0, 0, p_s0.shape[1])

    @pl.when(i % steps_per_seq == 0)
    def _():
        xbuf[0:PAD, :] = jnp.zeros((PAD, C), _F32)
        hcar[...] = jnp.zeros_like(hcar)

    data_lane = lax.broadcasted_iota(jnp.int32, (R, LANES), 1) < HEAD_DIM
    row = lax.broadcasted_iota(jnp.int32, (SUBLANES, C), 0)
    ones_rows = (lax.broadcasted_iota(jnp.int32, (VT_ROWS - HEAD_DIM, R), 0) == 0).astype(_BF16)
    half = C // 2

    def sigmoid(v):
        return 0.5 * jnp.tanh(0.5 * v) + 0.5

    def interleave(mxu_pieces, vpu_pieces):
        done = [0, 0]
        queues = [list(mxu_pieces), list(vpu_pieces)]
        while queues[0] or queues[1]:
            s = 0 if (queues[0] and (not queues[1] or done[0] <= done[1])) else 1
            cost, thunk = queues[s].pop(0)
            done[s] += cost
            thunk()

    lru_state = {"carry": hcar[...]}
    for ph, x_next_ref in enumerate((xa_ref, xb_ref)):
        cur, nxt = slots[ph], slots[1 - ph]
        row0 = ph * R
        xt = x_next_ref[...].astype(_BF16)

        def project_cols(c0, xt=xt, nxt=nxt):
            nxt[:, c0:c0 + PROJ_COL_CHUNK] = jnp.dot(xt, w_ref[:, c0:c0 + PROJ_COL_CHUNK],
                                                     preferred_element_type=_F32)

        def lru_front(c, cur=cur):
            xr = cur[c * SUB:(c + 1) * SUB, XR0:XR0 + C]
            xbuf[PAD + c * SUB:PAD + (c + 1) * SUB, :] = xr
            y = cb_ref[...] + cw_ref[CONV_WIDTH - 1:CONV_WIDTH, :] * xr
            for d in range(1, CONV_WIDTH):
                y = y + (cw_ref[CONV_WIDTH - 1 - d:CONV_WIDTH - d, :]
                         * xbuf[PAD - d + c * SUB:PAD - d + (c + 1) * SUB, :])
            if c == R // SUB - 1:
                xbuf[0:PAD, :] = xr[SUB - PAD:SUB, :]
            yb = y.astype(_BF16)

            def gate_lin(w3_ref, b_ref):
                parts = [jnp.dot(yb[:, s * half:(s + 1) * half], w3_ref[s],
                                 preferred_element_type=_F32) for s in range(2)]
                return jnp.concatenate(parts, axis=1) + b_ref[...]

            r = sigmoid(gate_lin(wrg_ref, brg_ref))
            ig = sigmoid(gate_lin(wig_ref, big_ref))
            lam = lam_ref[...]
            softplus_neg_lam = jnp.maximum(-lam, 0.0) + jnp.log(1.0 + jnp.exp(-jnp.abs(lam)))
            a = jnp.exp(r * ((-LRU_C) * softplus_neg_lam))
            v1 = 1.0 - a * a
            lru_state[c] = (a, jnp.where(v1 > 0.0, v1 * lax.rsqrt(v1), 0.0) * (ig * y))

        def lru_back(c, cur=cur, row0=row0):
            a, u = lru_state.pop(c)
            carry = lru_state["carry"]
            h_groups = []
            for g in range(SUB // SUBLANES):
                ag = a[g * SUBLANES:(g + 1) * SUBLANES, :]
                ug = u[g * SUBLANES:(g + 1) * SUBLANES, :]
                for d in (1, 2, 4):
                    a_sh = pltpu.roll(ag, d, 0)
                    u_sh = pltpu.roll(ug, d, 0)
                    m = row >= d
                    ug = jnp.where(m, ag * u_sh + ug, ug)
                    ag = jnp.where(m, ag * a_sh, ag)
                hg = ag * carry + ug
                h_groups.append(hg)
                carry = jnp.broadcast_to(hg[SUBLANES - 1:SUBLANES, :], (SUBLANES, C))
            lru_state["carry"] = carry
            gr = cur[c * SUB:(c + 1) * SUB, GR0:GR0 + C]
            gelu = 0.5 * gr * (1.0 + jnp.tanh(0.7978845608028654 * (gr + 0.044715 * (gr * gr * gr))))
            lru_ref[row0 + c * SUB:row0 + (c + 1) * SUB, :] = (
                jnp.concatenate(h_groups, axis=0) * gelu).astype(lru_ref.dtype)

        def head_pair(c0, pair, cur=cur):
            two = cur[:, c0 + pair * LANES:c0 + (pair + 1) * LANES]
            return ((pair * HEADS_PER_LANE_TILE, two),
                    (pair * HEADS_PER_LANE_TILE + 1, pltpu.roll(two, HEAD_DIM, 1)))

        def q_epilogue(pair, head_pair=head_pair, row0=row0):
            for h, qh in head_pair(Q0, pair):
                q_ref[row0:row0 + R, h * LANES:(h + 1) * LANES] = jnp.where(
                    data_lane, qh * (HEAD_DIM ** -0.5 * LOG2E), qx_ref[...]).astype(_BF16)

        def k_epilogue(pair, head_pair=head_pair, row0=row0, ph=ph):
            for h, kh in head_pair(K0, pair):
                cols = slice(h * LANES, (h + 1) * LANES)
                k_ref[row0:row0 + R, cols] = jnp.where(data_lane, kh.astype(_BF16),
                                                       kx_ref[row0:row0 + R, cols])
                kpad = jnp.where(data_lane, kh, 0.0)
                for s in range(R // MOBA_BLOCK):
                    km_ref[0, ph * (R // MOBA_BLOCK) + s, :, cols] = jnp.mean(
                        kpad[s * MOBA_BLOCK:(s + 1) * MOBA_BLOCK], axis=0, keepdims=True)

        def v_epilogue(pair, cur=cur, row0=row0):
            vt = cur[:, V0 + pair * LANES:V0 + (pair + 1) * LANES].T.astype(_BF16)
            for hh in range(HEADS_PER_LANE_TILE):
                r0 = (pair * HEADS_PER_LANE_TILE + hh) * VT_ROWS
                vt_ref[0, r0:r0 + HEAD_DIM, row0:row0 + R] = vt[hh * HEAD_DIM:(hh + 1) * HEAD_DIM, :]
                vt_ref[0, r0 + HEAD_DIM:r0 + VT_ROWS, row0:row0 + R] = ones_rows

        mxu_pieces = [(2 * PROJ_COL_CHUNK, functools.partial(project_cols, c0))
                      for c0 in range(0, nxt.shape[1], PROJ_COL_CHUNK)]
        vpu_pieces = []
        for c in range(R // SUB):
            vpu_pieces += [(760, functools.partial(lru_front, c)), (530, functools.partial(lru_back, c))]
        for pair in range(N_HEAD_PAIRS):
            vpu_pieces += [(80, functools.partial(q_epilogue, pair)),
                           (110, functools.partial(k_epilogue, pair)),
                           (40, functools.partial(v_epilogue, pair))]
        interleave(mxu_pieces, vpu_pieces)
    hcar[...] = lru_state["carry"]


def _proj_lru(x, w_in, qx, kx, conv_w, conv_b, wrg, b_rg, wig, b_ig, lam):
    B, S, D = x.shape
    C = LRU_WIDTH
    nb = S // MOBA_BLOCK
    R = PROJ_ROWS
    step_rows = PROJ_TILES_PER_STEP * R
    n_tiles = B * S // R
    steps_per_seq = S // step_rows
    assert S % step_rows == 0 and PROJ_TILES_PER_STEP == 2
    x2 = x.reshape(B * S, D)
    resident = lambda shape: pl.BlockSpec(shape, lambda i: (0,) * len(shape),
                                          pipeline_mode=pl.Buffered(1))
    tile = lambda offset: pl.BlockSpec(
        (R, D), lambda i: (jnp.minimum(PROJ_TILES_PER_STEP * i + offset, n_tiles - 1), 0))
    q, k, vt, km, lru = pl.pallas_call(
        functools.partial(_proj_kernel, steps_per_seq=steps_per_seq),
        grid=(n_tiles // PROJ_TILES_PER_STEP,),
        in_specs=[
            pl.BlockSpec((R, D), lambda i: (0, 0), pipeline_mode=pl.Buffered(1)),
            tile(1),
            tile(2),
            resident((D, 3 * ATTN_WIDTH + 2 * C)),
            resident((1, LANES)),
            pl.BlockSpec((step_rows, AUG_WIDTH), lambda i: (i % steps_per_seq, 0)),
            resident((CONV_WIDTH, C)),
            resident((1, C)),
            resident((2, C // 2, C // 2)),
            resident((1, C)),
            resident((2, C // 2, C // 2)),
            resident((1, C)),
            resident((1, C)),
        ],
        out_specs=[
            pl.BlockSpec((step_rows, AUG_WIDTH), lambda i: (i, 0)),
            pl.BlockSpec((step_rows, AUG_WIDTH), lambda i: (i, 0)),
            pl.BlockSpec((1, N_HEADS * VT_ROWS, step_rows),
                         lambda i: (i // steps_per_seq, 0, i % steps_per_seq)),
            pl.BlockSpec((1, step_rows // MOBA_BLOCK, 1, AUG_WIDTH),
                         lambda i: (i // steps_per_seq, i % steps_per_seq, 0, 0)),
            pl.BlockSpec((step_rows, C), lambda i: (i, 0)),
        ],
        out_shape=[
            jax.ShapeDtypeStruct((B * S, AUG_WIDTH), _BF16),
            jax.ShapeDtypeStruct((B * S, AUG_WIDTH), _BF16),
            jax.ShapeDtypeStruct((B, N_HEADS * VT_ROWS, S), _BF16),
            jax.ShapeDtypeStruct((B, nb, 1, AUG_WIDTH), _F32),
            jax.ShapeDtypeStruct((B * S, C), _BF16),
        ],
        scratch_shapes=[
            pltpu.VMEM((R, 3 * ATTN_WIDTH + 2 * C), _F32),
            pltpu.VMEM((R, 3 * ATTN_WIDTH + 2 * C), _F32),
            pltpu.VMEM((R + SUBLANES, C), _F32),
            pltpu.VMEM((SUBLANES, C), _F32),
        ],
        compiler_params=pltpu.CompilerParams(
            dimension_semantics=("arbitrary",),
            vmem_limit_bytes=VMEM_LIMIT_BYTES),
        name="proj_lru",
    )(x2, x2, x2, w_in, qx, kx, conv_w, conv_b, wrg, b_rg, wig, b_ig, lam)
    return (q.reshape(B, S, AUG_WIDTH), k.reshape(B, S, AUG_WIDTH), vt, km, lru)


def _aug_constants(seq_len):
    nb = seq_len // MOBA_BLOCK
    alibi_lane0 = SEL_LANE0 + nb
    lane = np.arange(LANES)
    qx = ((lane >= alibi_lane0) & (lane < alibi_lane0 + ALIBI_PIECES)).astype(np.float32).reshape(1, LANES)

    slopes = np.exp2(-8.0 * np.arange(1, N_HEADS + 1, dtype=np.float32) / N_HEADS).astype(np.float32)
    pos = np.arange(seq_len)
    onehot = (pos[:, None] // MOBA_BLOCK == np.arange(nb)[None, :]).astype(np.float32)
    val = (np.float32(LOG2E) * slopes)[None, :] * (pos[:, None] - seq_len // 2).astype(np.float32)

    def trunc_bf16(v):
        return (v.view(np.uint32) & np.uint32(0xFFFF0000)).view(np.float32)

    hi = trunc_bf16(val)
    mid = trunc_bf16(val - hi)
    lo = val - hi - mid
    pieces = np.stack([hi, mid, lo], axis=-1)
    extras = np.concatenate([
        np.zeros((seq_len, N_HEADS, HEAD_DIM), np.float32),
        np.broadcast_to(onehot[:, None, :], (seq_len, N_HEADS, nb)),
        pieces,
        np.zeros((seq_len, N_HEADS, LANES - alibi_lane0 - ALIBI_PIECES), np.float32)], axis=-1)
    return jnp.asarray(qx), jnp.asarray(extras.reshape(seq_len, AUG_WIDTH).astype(_BF16))


def _attn_kernel(q_ref, k_ref, vt_ref, km_ref, o_ref):
    blk = MOBA_BLOCK
    nb = k_ref.shape[1] // blk

    lane = lax.broadcasted_iota(jnp.int32, (blk, LANES), 1)
    sel_lane = (lane >= SEL_LANE0) & (lane < SEL_LANE0 + nb)
    key_row = lax.broadcasted_iota(jnp.int32, (blk, blk), 0)
    qry_col = lax.broadcasted_iota(jnp.int32, (blk, blk), 1)
    causal = key_row <= qry_col
    blk_idx = lax.broadcasted_iota(jnp.int32, (nb, blk), 0)

    def query_tile(j, hh):
        nk = (j + 1) * blk
        cols = slice(hh * LANES, (hh + 1) * LANES)
        qa = q_ref[0, j * blk:nk, cols]
        if j > MOBA_TOPK:
            kmh = km_ref[0, :, 0, cols].astype(_BF16)
            gate = lax.dot_general(kmh, qa, _NT_DIMS, preferred_element_type=_F32)
            rank = jnp.zeros((nb, blk), jnp.int32)
            for n2 in range(j):
                g2 = gate[n2:n2 + 1, :]
                beats = (g2 > gate) | ((g2 == gate) & (n2 < blk_idx))
                rank = rank + beats.astype(jnp.int32)
            keep = ((blk_idx < j) & (rank < MOBA_TOPK)) | (blk_idx >= j)
            selb = jnp.where(keep, 0.0, NEG_INF)
            selb_rows = jnp.concatenate(
                [jnp.zeros((SEL_LANE0, blk), _F32), selb,
                 jnp.zeros((LANES - SEL_LANE0 - nb, blk), _F32)], axis=0)
            qa = jnp.where(sel_lane, selb_rows.T.astype(_BF16), qa)
        return qa

    def scores(j, hh):
        nk = (j + 1) * blk
        cols = slice(hh * LANES, (hh + 1) * LANES)
        z = lax.dot_general(k_ref[0, 0:nk, cols], query_tile(j, hh), _NT_DIMS,
                            preferred_element_type=_F32)
        z_own = jnp.where(causal, z[nk - blk:, :], NEG_INF)
        z = z_own if j == 0 else jnp.concatenate([z[:nk - blk, :], z_own], axis=0)
        m = jnp.max(z, axis=0, keepdims=True)
        return z, m

    def probs(zm):
        z, m = zm
        return jnp.exp2(z - m).astype(_BF16)

    def weighted_values(j, hh, p):
        nk = (j + 1) * blk
        vt = vt_ref[0, hh * VT_ROWS:(hh + 1) * VT_ROWS, 0:nk]
        acc = jnp.dot(vt, p, preferred_element_type=_F32)
        return acc[0:HEAD_DIM, :] / acc[HEAD_DIM:HEAD_DIM + 1, :]

    items = [(j, hh) for j in range(nb) for hh in range(HEADS_PER_LANE_TILE)]
    n_items = len(items)
    zm, ps, outs = {}, {}, {}
    for t in range(-2 * ATTN_STAGE_LAG, n_items):
        if t + 2 * ATTN_STAGE_LAG < n_items:
            zm[t + 2 * ATTN_STAGE_LAG] = scores(*items[t + 2 * ATTN_STAGE_LAG])
        if 0 <= t + ATTN_STAGE_LAG < n_items:
            ps[t + ATTN_STAGE_LAG] = probs(zm.pop(t + ATTN_STAGE_LAG))
        if t >= 0:
            j, hh = items[t]
            outs[hh] = weighted_values(j, hh, ps.pop(t))
            if hh == HEADS_PER_LANE_TILE - 1:
                o_ref[0, j * blk:(j + 1) * blk, :] = jnp.concatenate(
                    [outs[h] for h in range(HEADS_PER_LANE_TILE)], axis=0).T.astype(o_ref.dtype)


def _moba_attention(q, k, vt, km):
    B, S, _ = q.shape
    nb = S // MOBA_BLOCK
    pair_w = HEADS_PER_LANE_TILE * LANES
    return pl.pallas_call(
        _attn_kernel,
        grid=(B, N_HEAD_PAIRS),
        in_specs=[
            pl.BlockSpec((1, S, pair_w), lambda b, h: (b, 0, h)),
            pl.BlockSpec((1, S, pair_w), lambda b, h: (b, 0, h)),
            pl.BlockSpec((1, HEADS_PER_LANE_TILE * VT_ROWS, S), lambda b, h: (b, h, 0)),
            pl.BlockSpec((1, nb, 1, pair_w), lambda b, h: (b, 0, 0, h)),
        ],
        out_specs=pl.BlockSpec((1, S, LANES), lambda b, h: (b, 0, h)),
        out_shape=jax.ShapeDtypeStruct((B, S, ATTN_WIDTH), _BF16),
        compiler_params=pltpu.CompilerParams(
            dimension_semantics=("parallel", "parallel"),
            vmem_limit_bytes=VMEM_LIMIT_BYTES),
        name="moba_attention",
    )(q, k, vt, km)


def _layer_norm(y, g, b):
    mu = jnp.mean(y, axis=-1, keepdims=True)
    yc = y - mu
    var = jnp.mean(yc * yc, axis=-1, keepdims=True)
    return yc * lax.rsqrt(var + LN_EPS) * g + b


def _mlp_kernel(x_ref, attn_ref, lru_ref, wo_ref, g1_ref, b1_ref, wup_ref, bup_ref,
                wdn_ref, bdn_ref, g2_ref, b2_ref, o_ref):
    n_sub = MLP_ROWS // MLP_SUB_ROWS
    n_ff = wup_ref.shape[1] // MLP_FF_TILE

    def rows(c):
        return slice(c * MLP_SUB_ROWS, (c + 1) * MLP_SUB_ROWS)

    def out_proj(c):
        mix = jnp.dot(attn_ref[rows(c), :], wo_ref[0:ATTN_WIDTH, :], preferred_element_type=_F32)
        return mix + jnp.dot(lru_ref[rows(c), :], wo_ref[ATTN_WIDTH:, :], preferred_element_type=_F32)

    def norm1(c, mix):
        x1 = _layer_norm(DEEPNORM_ALPHA * x_ref[rows(c), :] + mix, g1_ref[...], b1_ref[...])
        return x1, x1.astype(_BF16)

    def ff_tile(x1b, f):
        cols = slice(f * MLP_FF_TILE, (f + 1) * MLP_FF_TILE)
        h = jnp.dot(x1b, wup_ref[:, cols], preferred_element_type=_F32) + bup_ref[:, cols]
        h = jnp.maximum(h, 0.0)
        return jnp.dot((h * h).astype(_BF16), wdn_ref[cols, :], preferred_element_type=_F32)

    def norm2(c, x1, acc):
        y = DEEPNORM_ALPHA * x1 + acc + bdn_ref[...]
        o_ref[rows(c), :] = _layer_norm(y, g2_ref[...], b2_ref[...])

    x1, x1b = norm1(0, out_proj(0))
    pending = None
    for c in range(n_sub):
        nxt_mix = out_proj(c + 1) if c + 1 < n_sub else None
        nxt = None
        acc = None
        for f in range(n_ff):
            part = ff_tile(x1b, f)
            acc = part if acc is None else acc + part
            if f == 1 and nxt_mix is not None:
                nxt = norm1(c + 1, nxt_mix)
            if f == n_ff // 2 and pending is not None:
                norm2(*pending)
                pending = None
        pending = (c, x1, acc)
        if nxt is not None:
            x1, x1b = nxt
    norm2(*pending)


def _outproj_mlp(x2, attn2, lru2, w_out, g1, b1, w_up, b_up, w_down, b_down, g2, b2):
    M, D = x2.shape
    F = w_up.shape[1]
    tm = MLP_ROWS
    resident = lambda shape: pl.BlockSpec(shape, lambda i: (0, 0), pipeline_mode=pl.Buffered(1))
    return pl.pallas_call(
        _mlp_kernel,
        grid=(M // tm,),
        in_specs=[
            pl.BlockSpec((tm, D), lambda i: (i, 0)),
            pl.BlockSpec((tm, ATTN_WIDTH), lambda i: (i, 0)),
            pl.BlockSpec((tm, LRU_WIDTH), lambda i: (i, 0)),
            resident((D, D)),
            resident((1, D)), resident((1, D)),
            resident((D, F)),
            resident((1, F)),
            resident((F, D)),
            resident((1, D)), resident((1, D)), resident((1, D)),
        ],
        out_specs=pl.BlockSpec((tm, D), lambda i: (i, 0)),
        out_shape=jax.ShapeDtypeStruct((M, D), _F32),
        compiler_params=pltpu.CompilerParams(
            dimension_semantics=("parallel",),
            vmem_limit_bytes=VMEM_LIMIT_BYTES),
        name="outproj_mlp",
    )(x2, attn2, lru2, w_out, g1, b1, w_up, b_up, w_down, b_down, g2, b2)


def _block_diag_halves(w):
    n_half = N_LRU_BLOCKS // 2
    eye = jnp.eye(n_half, dtype=w.dtype)
    wh = w.reshape(2, n_half, LRU_BLOCK, LRU_BLOCK)
    full = jnp.einsum('snde,nm->sndme', wh, eye)
    return full.reshape(2, n_half * LRU_BLOCK, n_half * LRU_BLOCK)


def kernel(x, w_in, conv_w, conv_b, w_rg, b_rg, w_ig, b_ig, lru_lambda, w_out, ln1_g, ln1_b,
           w_up, b_up, w_down, b_down, ln2_g, ln2_b):
    B, S, D = x.shape
    depth = w_in.shape[0]
    row = lambda v: v.reshape(1, -1)
    qx, kx = _aug_constants(S)
    for l in range(depth):
        q, k, vt, km, lru = _proj_lru(
            x, w_in[l].astype(_BF16), qx, kx, conv_w[l], row(conv_b[l]),
            _block_diag_halves(w_rg[l]).astype(_BF16), row(b_rg[l]),
            _block_diag_halves(w_ig[l]).astype(_BF16), row(b_ig[l]),
            row(lru_lambda[l]))
        attn = _moba_attention(q, k, vt, km)
        out = _outproj_mlp(
            x.reshape(B * S, D), attn.reshape(B * S, ATTN_WIDTH), lru.reshape(B * S, LRU_WIDTH),
            w_out[l].astype(_BF16), row(ln1_g[l]), row(ln1_b[l]),
            w_up[l].astype(_BF16), row(b_up[l]), w_down[l].astype(_BF16), row(b_down[l]),
            row(ln2_g[l]), row(ln2_b[l]))
        x = out.reshape(B, S, D)
    return x
```

```python
import functools

import numpy as np

import jax
import jax.numpy as jnp
from jax import lax
from jax.experimental import pallas as pl
from jax.experimental.pallas import tpu as pltpu

D_MODEL = 1024
ATTN_WIDTH = 512
N_HEADS = 8
HEAD_DIM = 64
LRU_WIDTH = 512
N_LRU_BLOCKS = 8
LRU_BLOCK = 64
CONV_WIDTH = 4
LRU_C = 8.0
MOBA_BLOCK = 256
MOBA_TOPK = 3
D_FF = 4096
LN_EPS = 1e-5
DEEPNORM_ALPHA = 2.0 ** 0.25
NEG_INF = -1e30

SUBLANES = 8
LANES = 128
HEADS_PER_LANE_TILE = LANES // HEAD_DIM
N_HEAD_PAIRS = N_HEADS // HEADS_PER_LANE_TILE
AUG_WIDTH = N_HEADS * LANES
SEL_LANE0 = HEAD_DIM
ALIBI_PIECES = 3
VT_ROWS = HEAD_DIM + 16
LOG2E = 1.4426950408889634

ATTN_STAGE_LAG = 2
PROJ_ROWS = 512
PROJ_TILES_PER_STEP = 2
PROJ_COL_CHUNK = 256
PROJ_SUB_ROWS = 128
MLP_ROWS = 512
MLP_SUB_ROWS = 256
MLP_FF_TILE = 1024
VMEM_LIMIT_BYTES = 56 * 1024 * 1024

_F32 = jnp.float32
_BF16 = jnp.bfloat16
_NT_DIMS = (((1,), (1,)), ((), ()))


def _proj_kernel(x0_ref, xa_ref, xb_ref, w_ref, qx_ref, kx_ref, cw_ref, cb_ref, wrg_ref, brg_ref,
                 wig_ref, big_ref, lam_ref, q_ref, k_ref, vt_ref, km_ref, lru_ref,
                 p_s0, p_s1, xbuf, hcar, *, steps_per_seq):
    i = pl.program_id(0)
    R = PROJ_ROWS
    C = LRU_WIDTH
    SUB = PROJ_SUB_ROWS
    PAD = SUBLANES
    Q0, K0, V0, XR0, GR0 = 0, ATTN_WIDTH, 2 * ATTN_WIDTH, 3 * ATTN_WIDTH, 3 * ATTN_WIDTH + C

    slots = (p_s0, p_s1)

    def project(x_tile_ref, slot_ref, c0, width):
        xt = x_tile_ref[...].astype(_BF16)
        slot_ref[:, c0:c0 + width] = jnp.dot(xt, w_ref[:, c0:c0 + width],
                                             preferred_element_type=_F32)

    @pl.when(i == 0)
    def _():
        project(x0_ref, p_s0, 0, p_s0.shape[1])

    @pl.when(i % steps_per_seq == 0)
    def _():
        xbuf[0:PAD, :] = jnp.zeros((PAD, C), _F32)
        hcar[...] = jnp.zeros_like(hcar)

    data_lane = lax.broadcasted_iota(jnp.int32, (R, LANES), 1) < HEAD_DIM
    row = lax.broadcasted_iota(jnp.int32, (SUBLANES, C), 0)
    ones_rows = (lax.broadcasted_iota(jnp.int32, (VT_ROWS - HEAD_DIM, R), 0) == 0).astype(_BF16)
    half = C // 2

    def sigmoid(v):
        return 0.5 * jnp.tanh(0.5 * v) + 0.5

    def interleave(mxu_pieces, vpu_pieces):
        done = [0, 0]
        queues = [list(mxu_pieces), list(vpu_pieces)]
        while queues[0] or queues[1]:
            s = 0 if (queues[0] and (not queues[1] or done[0] <= done[1])) else 1
            cost, thunk = queues[s].pop(0)
            done[s] += cost
            thunk()

    lru_state = {"carry": hcar[...]}
    for ph, x_next_ref in enumerate((xa_ref, xb_ref)):
        cur, nxt = slots[ph], slots[1 - ph]
        row0 = ph * R
        xt = x_next_ref[...].astype(_BF16)

        def project_cols(c0, xt=xt, nxt=nxt):
            nxt[:, c0:c0 + PROJ_COL_CHUNK] = jnp.dot(xt, w_ref[:, c0:c0 + PROJ_COL_CHUNK],
                                                     preferred_element_type=_F32)

        def lru_front(c, cur=cur):
            xr = cur[c * SUB:(c + 1) * SUB, XR0:XR0 + C]
            xbuf[PAD + c * SUB:PAD + (c + 1) * SUB, :] = xr
            y = cb_ref[...] + cw_ref[CONV_WIDTH - 1:CONV_WIDTH, :] * xr
            for d in range(1, CONV_WIDTH):
                y = y + (cw_ref[CONV_WIDTH - 1 - d:CONV_WIDTH - d, :]
                         * xbuf[PAD - d + c * SUB:PAD - d + (c + 1) * SUB, :])
            if c == R // SUB - 1:
                xbuf[0:PAD, :] = xr[SUB - PAD:SUB, :]
            yb = y.astype(_BF16)

            def gate_lin(w3_ref, b_ref):
                parts = [jnp.dot(yb[:, s * half:(s + 1) * half], w3_ref[s],
                                 preferred_element_type=_F32) for s in range(2)]
                return jnp.concatenate(parts, axis=1) + b_ref[...]

            r = sigmoid(gate_lin(wrg_ref, brg_ref))
            ig = sigmoid(gate_lin(wig_ref, big_ref))
            lam = lam_ref[...]
            softplus_neg_lam = jnp.maximum(-lam, 0.0) + jnp.log(1.0 + jnp.exp(-jnp.abs(lam)))
            a = jnp.exp(r * ((-LRU_C) * softplus_neg_lam))
            v1 = 1.0 - a * a
            lru_state[c] = (a, jnp.where(v1 > 0.0, v1 * lax.rsqrt(v1), 0.0) * (ig * y))

        def lru_back(c, cur=cur, row0=row0):
            a, u = lru_state.pop(c)
            carry = lru_state["carry"]
            h_groups = []
            for g in range(SUB // SUBLANES):
                ag = a[g * SUBLANES:(g + 1) * SUBLANES, :]
                ug = u[g * SUBLANES:(g + 1) * SUBLANES, :]
                for d in (1, 2, 4):
                    a_sh = pltpu.roll(ag, d, 0)
                    u_sh = pltpu.roll(ug, d, 0)
                    m = row >= d
                    ug = jnp.where(m, ag * u_sh + ug, ug)
                    ag = jnp.where(m, ag * a_sh, ag)
                hg = ag * carry + ug
                h_groups.append(hg)
                carry = jnp.broadcast_to(hg[SUBLANES - 1:SUBLANES, :], (SUBLANES, C))
            lru_state["carry"] = carry
            gr = cur[c * SUB:(c + 1) * SUB, GR0:GR0 + C]
            gelu = 0.5 * gr * (1.0 + jnp.tanh(0.7978845608028654 * (gr + 0.044715 * (gr * gr * gr))))
            lru_ref[row0 + c * SUB:row0 + (c + 1) * SUB, :] = (
                jnp.concatenate(h_groups, axis=0) * gelu).astype(lru_ref.dtype)

        def head_pair(c0, pair, cur=cur):
            two = cur[:, c0 + pair * LANES:c0 + (pair + 1) * LANES]
            return ((pair * HEADS_PER_LANE_TILE, two),
                    (pair * HEADS_PER_LANE_TILE + 1, pltpu.roll(two, HEAD_DIM, 1)))

        def q_epilogue(pair, head_pair=head_pair, row0=row0):
            for h, qh in head_pair(Q0, pair):
                q_ref[row0:row0 + R, h * LANES:(h + 1) * LANES] = jnp.where(
                    data_lane, qh * (HEAD_DIM ** -0.5 * LOG2E), qx_ref[...]).astype(_BF16)

        def k_epilogue(pair, head_pair=head_pair, row0=row0, ph=ph):
            for h, kh in head_pair(K0, pair):
                cols = slice(h * LANES, (h + 1) * LANES)
                k_ref[row0:row0 + R, cols] = jnp.where(data_lane, kh.astype(_BF16),
                                                       kx_ref[row0:row0 + R, cols])
                kpad = jnp.where(data_lane, kh, 0.0)
                for s in range(R // MOBA_BLOCK):
                    km_ref[0, ph * (R // MOBA_BLOCK) + s, :, cols] = jnp.mean(
                        kpad[s * MOBA_BLOCK:(s + 1) * MOBA_BLOCK], axis=0, keepdims=True)

        def v_epilogue(pair, cur=cur, row0=row0):
            vt = cur[:, V0 + pair * LANES:V0 + (pair + 1) * LANES].T.astype(_BF16)
            for hh in range(HEADS_PER_LANE_TILE):
                r0 = (pair * HEADS_PER_LANE_TILE + hh) * VT_ROWS
                vt_ref[0, r0:r0 + HEAD_DIM, row0:row0 + R] = vt[hh * HEAD_DIM:(hh + 1) * HEAD_DIM, :]
                vt_ref[0, r0 + HEAD_DIM:r0 + VT_ROWS, row0:row0 + R] = ones_rows

        mxu_pieces = [(2 * PROJ_COL_CHUNK, functools.partial(project_cols, c0))
                      for c0 in range(0, nxt.shape[1], PROJ_COL_CHUNK)]
        vpu_pieces = []
        for c in range(R // SUB):
            vpu_pieces += [(760, functools.partial(lru_front, c)), (530, functools.partial(lru_back, c))]
        for pair in range(N_HEAD_PAIRS):
            vpu_pieces += [(80, functools.partial(q_epilogue, pair)),
                           (110, functools.partial(k_epilogue, pair)),
                           (40, functools.partial(v_epilogue, pair))]
        interleave(mxu_pieces, vpu_pieces)
    hcar[...] = lru_state["carry"]


def _proj_lru(x, w_in, qx, kx, conv_w, conv_b, wrg, b_rg, wig, b_ig, lam):
    B, S, D = x.shape
    C = LRU_WIDTH
    nb = S // MOBA_BLOCK
    R = PROJ_ROWS
    step_rows = PROJ_TILES_PER_STEP * R
    n_tiles = B * S // R
    steps_per_seq = S // step_rows
    assert S % step_rows == 0 and PROJ_TILES_PER_STEP == 2
    x2 = x.reshape(B * S, D)
    resident = lambda shape: pl.BlockSpec(shape, lambda i: (0,) * len(shape),
                                          pipeline_mode=pl.Buffered(1))
    tile = lambda offset: pl.BlockSpec(
        (R, D), lambda i: (jnp.minimum(PROJ_TILES_PER_STEP * i + offset, n_tiles - 1), 0))
    q, k, vt, km, lru = pl.pallas_call(
        functools.partial(_proj_kernel, steps_per_seq=steps_per_seq),
        grid=(n_tiles // PROJ_TILES_PER_STEP,),
        in_specs=[
            pl.BlockSpec((R, D), lambda i: (0, 0), pipeline_mode=pl.Buffered(1)),
            tile(1),
            tile(2),
            resident((D, 3 * ATTN_WIDTH + 2 * C)),
            resident((1, LANES)),
            pl.BlockSpec((step_rows, AUG_WIDTH), lambda i: (i % steps_per_seq, 0)),
            resident((CONV_WIDTH, C)),
            resident((1, C)),
            resident((2, C // 2, C // 2)),
            resident((1, C)),
            resident((2, C // 2, C // 2)),
            resident((1, C)),
            resident((1, C)),
        ],
        out_specs=[
            pl.BlockSpec((step_rows, AUG_WIDTH), lambda i: (i, 0)),
            pl.BlockSpec((step_rows, AUG_WIDTH), lambda i: (i, 0)),
            pl.BlockSpec((1, N_HEADS * VT_ROWS, step_rows),
                         lambda i: (i // steps_per_seq, 0, i % steps_per_seq)),
            pl.BlockSpec((1, step_rows // MOBA_BLOCK, 1, AUG_WIDTH),
                         lambda i: (i // steps_per_seq, i % steps_per_seq, 0, 0)),
            pl.BlockSpec((step_rows, C), lambda i: (i, 0)),
        ],
        out_shape=[
            jax.ShapeDtypeStruct((B * S, AUG_WIDTH), _BF16),
            jax.ShapeDtypeStruct((B * S, AUG_WIDTH), _BF16),
            jax.ShapeDtypeStruct((B, N_HEADS * VT_ROWS, S), _BF16),
            jax.ShapeDtypeStruct((B, nb, 1, AUG_WIDTH), _F32),
            jax.ShapeDtypeStruct((B * S, C), _BF16),
        ],
        scratch_shapes=[
            pltpu.VMEM((R, 3 * ATTN_WIDTH + 2 * C), _F32),
            pltpu.VMEM((R, 3 * ATTN_WIDTH + 2 * C), _F32),
            pltpu.VMEM((R + SUBLANES, C), _F32),
            pltpu.VMEM((SUBLANES, C), _F32),
        ],
        compiler_params=pltpu.CompilerParams(
            dimension_semantics=("arbitrary",),
            vmem_limit_bytes=VMEM_LIMIT_BYTES),
        name="proj_lru",
    )(x2, x2, x2, w_in, qx, kx, conv_w, conv_b, wrg, b_rg, wig, b_ig, lam)
    return (q.reshape(B, S, AUG_WIDTH), k.reshape(B, S, AUG_WIDTH), vt, km, lru)


def _aug_constants(seq_len):
    nb = seq_len // MOBA_BLOCK
    alibi_lane0 = SEL_LANE0 + nb
    lane = np.arange(LANES)
    qx = ((lane >= alibi_lane0) & (lane < alibi_lane0 + ALIBI_PIECES)).astype(np.float32).reshape(1, LANES)

    slopes = np.exp2(-8.0 * np.arange(1, N_HEADS + 1, dtype=np.float32) / N_HEADS).astype(np.float32)
    pos = np.arange(seq_len)
    onehot = (pos[:, None] // MOBA_BLOCK == np.arange(nb)[None, :]).astype(np.float32)
    val = (np.float32(LOG2E) * slopes)[None, :] * (pos[:, None] - seq_len // 2).astype(np.float32)

    def trunc_bf16(v):
        return (v.view(np.uint32) & np.uint32(0xFFFF0000)).view(np.float32)

    hi = trunc_bf16(val)
    mid = trunc_bf16(val - hi)
    lo = val - hi - mid
    pieces = np.stack([hi, mid, lo], axis=-1)
    extras = np.concatenate([
        np.zeros((seq_len, N_HEADS, HEAD_DIM), np.float32),
        np.broadcast_to(onehot[:, None, :], (seq_len, N_HEADS, nb)),
        pieces,
        np.zeros((seq_len, N_HEADS, LANES - alibi_lane0 - ALIBI_PIECES), np.float32)], axis=-1)
    return jnp.asarray(qx), jnp.asarray(extras.reshape(seq_len, AUG_WIDTH).astype(_BF16))


def _attn_kernel(q_ref, k_ref, vt_ref, km_ref, o_ref):
    blk = MOBA_BLOCK
    nb = k_ref.shape[1] // blk

    lane = lax.broadcasted_iota(jnp.int32, (blk, LANES), 1)
    sel_lane = (lane >= SEL_LANE0) & (lane < SEL_LANE0 + nb)
    key_row = lax.broadcasted_iota(jnp.int32, (blk, blk), 0)
    qry_col = lax.broadcasted_iota(jnp.int32, (blk, blk), 1)
    causal = key_row <= qry_col
    blk_idx = lax.broadcasted_iota(jnp.int32, (nb, blk), 0)

    def query_tile(j, hh):
        nk = (j + 1) * blk
        cols = slice(hh * LANES, (hh + 1) * LANES)
        qa = q_ref[0, j * blk:nk, cols]
        if j > MOBA_TOPK:
            kmh = km_ref[0, :, 0, cols].astype(_BF16)
            gate = lax.dot_general(kmh, qa, _NT_DIMS, preferred_element_type=_F32)
            rank = jnp.zeros((nb, blk), jnp.int32)
            for n2 in range(j):
                g2 = gate[n2:n2 + 1, :]
                beats = (g2 > gate) | ((g2 == gate) & (n2 < blk_idx))
                rank = rank + beats.astype(jnp.int32)
            keep = ((blk_idx < j) & (rank < MOBA_TOPK)) | (blk_idx >= j)
            selb = jnp.where(keep, 0.0, NEG_INF)
            selb_rows = jnp.concatenate(
                [jnp.zeros((SEL_LANE0, blk), _F32), selb,
                 jnp.zeros((LANES - SEL_LANE0 - nb, blk), _F32)], axis=0)
            qa = jnp.where(sel_lane, selb_rows.T.astype(_BF16), qa)
        return qa

    def scores(j, hh):
        nk = (j + 1) * blk
        cols = slice(hh * LANES, (hh + 1) * LANES)
        z = lax.dot_general(k_ref[0, 0:nk, cols], query_tile(j, hh), _NT_DIMS,
                            preferred_element_type=_F32)
        z_own = jnp.where(causal, z[nk - blk:, :], NEG_INF)
        z = z_own if j == 0 else jnp.concatenate([z[:nk - blk, :], z_own], axis=0)
        m = jnp.max(z, axis=0, keepdims=True)
        return z, m

    def probs(zm):
        z, m = zm
        return jnp.exp2(z - m).astype(_BF16)

    def weighted_values(j, hh, p):
        nk = (j + 1) * blk
        vt = vt_ref[0, hh * VT_ROWS:(hh + 1) * VT_ROWS, 0:nk]
        acc = jnp.dot(vt, p, preferred_element_type=_F32)
        return acc[0:HEAD_DIM, :] / acc[HEAD_DIM:HEAD_DIM + 1, :]

    items = [(j, hh) for j in range(nb) for hh in range(HEADS_PER_LANE_TILE)]
    n_items = len(items)
    zm, ps, outs = {}, {}, {}
    for t in range(-2 * ATTN_STAGE_LAG, n_items):
        if t + 2 * ATTN_STAGE_LAG < n_items:
            zm[t + 2 * ATTN_STAGE_LAG] = scores(*items[t + 2 * ATTN_STAGE_LAG])
        if 0 <= t + ATTN_STAGE_LAG < n_items:
            ps[t + ATTN_STAGE_LAG] = probs(zm.pop(t + ATTN_STAGE_LAG))
        if t >= 0:
            j, hh = items[t]
            outs[hh] = weighted_values(j, hh, ps.pop(t))
            if hh == HEADS_PER_LANE_TILE - 1:
                o_ref[0, j * blk:(j + 1) * blk, :] = jnp.concatenate(
                    [outs[h] for h in range(HEADS_PER_LANE_TILE)], axis=0).T.astype(o_ref.dtype)


def _moba_attention(q, k, vt, km):
    B, S, _ = q.shape
    nb = S // MOBA_BLOCK
    pair_w = HEADS_PER_LANE_TILE * LANES
    return pl.pallas_call(
        _attn_kernel,
        grid=(B, N_HEAD_PAIRS),
        in_specs=[
            pl.BlockSpec((1, S, pair_w), lambda b, h: (b, 0, h)),
            pl.BlockSpec((1, S, pair_w), lambda b, h: (b, 0, h)),
            pl.BlockSpec((1, HEADS_PER_LANE_TILE * VT_ROWS, S), lambda b, h: (b, h, 0)),
            pl.BlockSpec((1, nb, 1, pair_w), lambda b, h: (b, 0, 0, h)),
        ],
        out_specs=pl.BlockSpec((1, S, LANES), lambda b, h: (b, 0, h)),
        out_shape=jax.ShapeDtypeStruct((B, S, ATTN_WIDTH), _BF16),
        compiler_params=pltpu.CompilerParams(
            dimension_semantics=("parallel", "parallel"),
            vmem_limit_bytes=VMEM_LIMIT_BYTES),
        name="moba_attention",
    )(q, k, vt, km)


def _layer_norm(y, g, b):
    mu = jnp.mean(y, axis=-1, keepdims=True)
    yc = y - mu
    var = jnp.mean(yc * yc, axis=-1, keepdims=True)
    return yc * lax.rsqrt(var + LN_EPS) * g + b


def _mlp_kernel(x_ref, attn_ref, lru_ref, wo_ref, g1_ref, b1_ref, wup_ref, bup_ref,
                wdn_ref, bdn_ref, g2_ref, b2_ref, o_ref):
    n_sub = MLP_ROWS // MLP_SUB_ROWS
    n_ff = wup_ref.shape[1] // MLP_FF_TILE

    def rows(c):
        return slice(c * MLP_SUB_ROWS, (c + 1) * MLP_SUB_ROWS)

    def out_proj(c):
        mix = jnp.dot(attn_ref[rows(c), :], wo_ref[0:ATTN_WIDTH, :], preferred_element_type=_F32)
        return mix + jnp.dot(lru_ref[rows(c), :], wo_ref[ATTN_WIDTH:, :], preferred_element_type=_F32)

    def norm1(c, mix):
        x1 = _layer_norm(DEEPNORM_ALPHA * x_ref[rows(c), :] + mix, g1_ref[...], b1_ref[...])
        return x1, x1.astype(_BF16)

    def ff_tile(x1b, f):
        cols = slice(f * MLP_FF_TILE, (f + 1) * MLP_FF_TILE)
        h = jnp.dot(x1b, wup_ref[:, cols], preferred_element_type=_F32) + bup_ref[:, cols]
        h = jnp.maximum(h, 0.0)
        return jnp.dot((h * h).astype(_BF16), wdn_ref[cols, :], preferred_element_type=_F32)

    def norm2(c, x1, acc):
        y = DEEPNORM_ALPHA * x1 + acc + bdn_ref[...]
        o_ref[rows(c), :] = _layer_norm(y, g2_ref[...], b2_ref[...])

    x1, x1b = norm1(0, out_proj(0))
    pending = None
    for c in range(n_sub):
        nxt_mix = out_proj(c + 1) if c + 1 < n_sub else None
        nxt = None
        acc = None
        for f in range(n_ff):
            part = ff_tile(x1b, f)
            acc = part if acc is None else acc + part
            if f == 1 and nxt_mix is not None:
                nxt = norm1(c + 1, nxt_mix)
            if f == n_ff // 2 and pending is not None:
                norm2(*pending)
                pending = None
        pending = (c, x1, acc)
        if nxt is not None:
            x1, x1b = nxt
    norm2(*pending)


def _outproj_mlp(x2, attn2, lru2, w_out, g1, b1, w_up, b_up, w_down, b_down, g2, b2):
    M, D = x2.shape
    F = w_up.shape[1]
    tm = MLP_ROWS
    resident = lambda shape: pl.BlockSpec(shape, lambda i: (0, 0), pipeline_mode=pl.Buffered(1))
    return pl.pallas_call(
        _mlp_kernel,
        grid=(M // tm,),
        in_specs=[
            pl.BlockSpec((tm, D), lambda i: (i, 0)),
            pl.BlockSpec((tm, ATTN_WIDTH), lambda i: (i, 0)),
            pl.BlockSpec((tm, LRU_WIDTH), lambda i: (i, 0)),
            resident((D, D)),
            resident((1, D)), resident((1, D)),
            resident((D, F)),
            resident((1, F)),
            resident((F, D)),
            resident((1, D)), resident((1, D)), resident((1, D)),
        ],
        out_specs=pl.BlockSpec((tm, D), lambda i: (i, 0)),
        out_shape=jax.ShapeDtypeStruct((M, D), _F32),
        compiler_params=pltpu.CompilerParams(
            dimension_semantics=("parallel",),
            vmem_limit_bytes=VMEM_LIMIT_BYTES),
        name="outproj_mlp",
    )(x2, attn2, lru2, w_out, g1, b1, w_up, b_up, w_down, b_down, g2, b2)


def _block_diag_halves(w):
    n_half = N_LRU_BLOCKS // 2
    eye = jnp.eye(n_half, dtype=w.dtype)
    wh = w.reshape(2, n_half, LRU_BLOCK, LRU_BLOCK)
    full = jnp.einsum('snde,nm->sndme', wh, eye)
    return full.reshape(2, n_half * LRU_BLOCK, n_half * LRU_BLOCK)


def kernel(x, w_in, conv_w, conv_b, w_rg, b_rg, w_ig, b_ig, lru_lambda, w_out, ln1_g, ln1_b,
           w_up, b_up, w_down, b_down, ln2_g, ln2_b):
    B, S, D = x.shape
    depth = w_in.shape[0]
    row = lambda v: v.reshape(1, -1)
    qx, kx = _aug_constants(S)
    for l in range(depth):
        q, k, vt, km, lru = _proj_lru(
            x, w_in[l].astype(_BF16), qx, kx, conv_w[l], row(conv_b[l]),
            _block_diag_halves(w_rg[l]).astype(_BF16), row(b_rg[l]),
            _block_diag_halves(w_ig[l]).astype(_BF16), row(b_ig[l]),
            row(lru_lambda[l]))
        attn = _moba_attention(q, k, vt, km)
        out = _outproj_mlp(
            x.reshape(B * S, D), attn.reshape(B * S, ATTN_WIDTH), lru.reshape(B * S, LRU_WIDTH),
            w_out[l].astype(_BF16), row(ln1_g[l]), row(ln1_b[l]),
            w_up[l].astype(_BF16), row(b_up[l]), w_down[l].astype(_BF16), row(b_down[l]),
            row(ln2_g[l]), row(ln2_b[l]))
        x = out.reshape(B, S, D)
    return x
```

```python
import functools

import numpy as np

import jax
import jax.numpy as jnp
from jax import lax
from jax.experimental import pallas as pl
from jax.experimental.pallas import tpu as pltpu

D_MODEL = 1024
ATTN_WIDTH = 512
N_HEADS = 8
HEAD_DIM = 64
LRU_WIDTH = 512
N_LRU_BLOCKS = 8
LRU_BLOCK = 64
CONV_WIDTH = 4
LRU_C = 8.0
MOBA_BLOCK = 256
MOBA_TOPK = 3
D_FF = 4096
LN_EPS = 1e-5
DEEPNORM_ALPHA = 2.0 ** 0.25
NEG_INF = -1e30

SUBLANES = 8
LANES = 128
HEADS_PER_LANE_TILE = LANES // HEAD_DIM
N_HEAD_PAIRS = N_HEADS // HEADS_PER_LANE_TILE
AUG_WIDTH = N_HEADS * LANES
SEL_LANE0 = HEAD_DIM
ALIBI_PIECES = 3
VT_ROWS = HEAD_DIM + 16
LOG2E = 1.4426950408889634

ATTN_STAGE_LAG = 2
PROJ_ROWS = 512
PROJ_TILES_PER_STEP = 2
PROJ_COL_CHUNK = 256
PROJ_SUB_ROWS = 128
MLP_ROWS = 1024
MLP_SUB_ROWS = 512
MLP_FF_TILE = 1024
VMEM_LIMIT_BYTES = 60 * 1024 * 1024

_F32 = jnp.float32
_BF16 = jnp.bfloat16
_NT_DIMS = (((1,), (1,)), ((), ()))


def _proj_kernel(x0_ref, xa_ref, xb_ref, w_ref, qx_ref, kx_ref, cw_ref, cb_ref, wrg_ref, brg_ref,
                 wig_ref, big_ref, lam_ref, q_ref, k_ref, vt_ref, km_ref, lru_ref,
                 p_s0, p_s1, xbuf, hcar, *, steps_per_seq):
    i = pl.program_id(0)
    R = PROJ_ROWS
    C = LRU_WIDTH
    SUB = PROJ_SUB_ROWS
    PAD = SUBLANES
    Q0, K0, V0, XR0, GR0 = 0, ATTN_WIDTH, 2 * ATTN_WIDTH, 3 * ATTN_WIDTH, 3 * ATTN_WIDTH + C

    slots = (p_s0, p_s1)

    def project(x_tile_ref, slot_ref, c0, width):
        xt = x_tile_ref[...].astype(_BF16)
        slot_ref[:, c0:c0 + width] = jnp.dot(xt, w_ref[:, c0:c0 + width],
                                             preferred_element_type=_F32)

    @pl.when(i == 0)
    def _():
        project(x0_ref, p_s0, 0, p_s0.shape[1])

    @pl.when(i % steps_per_seq == 0)
    def _():
        xbuf[0:PAD, :] = jnp.zeros((PAD, C), _F32)
        hcar[...] = jnp.zeros_like(hcar)

    data_lane = lax.broadcasted_iota(jnp.int32, (R, LANES), 1) < HEAD_DIM
    row = lax.broadcasted_iota(jnp.int32, (SUBLANES, C), 0)
    ones_rows = (lax.broadcasted_iota(jnp.int32, (VT_ROWS - HEAD_DIM, R), 0) == 0).astype(_BF16)
    half = C // 2

    def sigmoid(v):
        return 0.5 * jnp.tanh(0.5 * v) + 0.5

    def interleave(mxu_pieces, vpu_pieces):
        done = [0, 0]
        queues = [list(mxu_pieces), list(vpu_pieces)]
        while queues[0] or queues[1]:
            s = 0 if (queues[0] and (not queues[1] or done[0] <= done[1])) else 1
            cost, thunk = queues[s].pop(0)
            done[s] += cost
            thunk()

    lru_state = {"carry": hcar[...]}
    for ph, x_next_ref in enumerate((xa_ref, xb_ref)):
        cur, nxt = slots[ph], slots[1 - ph]
        row0 = ph * R
        xt = x_next_ref[...].astype(_BF16)

        def project_cols(c0, xt=xt, nxt=nxt):
            nxt[:, c0:c0 + PROJ_COL_CHUNK] = jnp.dot(xt, w_ref[:, c0:c0 + PROJ_COL_CHUNK],
                                                     preferred_element_type=_F32)

        def lru_front(c, cur=cur):
            xr = cur[c * SUB:(c + 1) * SUB, XR0:XR0 + C]
            xbuf[PAD + c * SUB:PAD + (c + 1) * SUB, :] = xr
            y = cb_ref[...] + cw_ref[CONV_WIDTH - 1:CONV_WIDTH, :] * xr
            for d in range(1, CONV_WIDTH):
                y = y + (cw_ref[CONV_WIDTH - 1 - d:CONV_WIDTH - d, :]
                         * xbuf[PAD - d + c * SUB:PAD - d + (c + 1) * SUB, :])
            if c == R // SUB - 1:
                xbuf[0:PAD, :] = xr[SUB - PAD:SUB, :]
            yb = y.astype(_BF16)

            def gate_lin(w3_ref, b_ref):
                parts = [jnp.dot(yb[:, s * half:(s + 1) * half], w3_ref[s],
                                 preferred_element_type=_F32) for s in range(2)]
                return jnp.concatenate(parts, axis=1) + b_ref[...]

            r = sigmoid(gate_lin(wrg_ref, brg_ref))
            ig = sigmoid(gate_lin(wig_ref, big_ref))
            lam = lam_ref[...]
            softplus_neg_lam = jnp.maximum(-lam, 0.0) + jnp.log(1.0 + jnp.exp(-jnp.abs(lam)))
            a = jnp.exp(r * ((-LRU_C) * softplus_neg_lam))
            v1 = 1.0 - a * a
            lru_state[c] = (a, jnp.where(v1 > 0.0, v1 * lax.rsqrt(v1), 0.0) * (ig * y))

        def lru_back(c, cur=cur, row0=row0):
            a, u = lru_state.pop(c)
            carry = lru_state["carry"]
            h_groups = []
            for g in range(SUB // SUBLANES):
                ag = a[g * SUBLANES:(g + 1) * SUBLANES, :]
                ug = u[g * SUBLANES:(g + 1) * SUBLANES, :]
                for d in (1, 2, 4):
                    a_sh = pltpu.roll(ag, d, 0)
                    u_sh = pltpu.roll(ug, d, 0)
                    m = row >= d
                    ug = jnp.where(m, ag * u_sh + ug, ug)
                    ag = jnp.where(m, ag * a_sh, ag)
                hg = ag * carry + ug
                h_groups.append(hg)
                carry = jnp.broadcast_to(hg[SUBLANES - 1:SUBLANES, :], (SUBLANES, C))
            lru_state["carry"] = carry
            gr = cur[c * SUB:(c + 1) * SUB, GR0:GR0 + C]
            gelu = 0.5 * gr * (1.0 + jnp.tanh(0.7978845608028654 * (gr + 0.044715 * (gr * gr * gr))))
            lru_ref[row0 + c * SUB:row0 + (c + 1) * SUB, :] = (
                jnp.concatenate(h_groups, axis=0) * gelu).astype(lru_ref.dtype)

        def head_pair(c0, pair, cur=cur):
            two = cur[:, c0 + pair * LANES:c0 + (pair + 1) * LANES]
            return ((pair * HEADS_PER_LANE_TILE, two),
                    (pair * HEADS_PER_LANE_TILE + 1, pltpu.roll(two, HEAD_DIM, 1)))

        def q_epilogue(pair, head_pair=head_pair, row0=row0):
            for h, qh in head_pair(Q0, pair):
                q_ref[row0:row0 + R, h * LANES:(h + 1) * LANES] = jnp.where(
                    data_lane, qh * (HEAD_DIM ** -0.5 * LOG2E), qx_ref[...]).astype(_BF16)

        def k_epilogue(pair, head_pair=head_pair, row0=row0, ph=ph):
            for h, kh in head_pair(K0, pair):
                cols = slice(h * LANES, (h + 1) * LANES)
                k_ref[row0:row0 + R, cols] = jnp.where(data_lane, kh.astype(_BF16),
                                                       kx_ref[row0:row0 + R, cols])
                kpad = jnp.where(data_lane, kh, 0.0)
                for s in range(R // MOBA_BLOCK):
                    km_ref[0, ph * (R // MOBA_BLOCK) + s, :, cols] = jnp.mean(
                        kpad[s * MOBA_BLOCK:(s + 1) * MOBA_BLOCK], axis=0, keepdims=True)

        def v_epilogue(pair, cur=cur, row0=row0):
            vt = cur[:, V0 + pair * LANES:V0 + (pair + 1) * LANES].T.astype(_BF16)
            for hh in range(HEADS_PER_LANE_TILE):
                r0 = (pair * HEADS_PER_LANE_TILE + hh) * VT_ROWS
                vt_ref[0, r0:r0 + HEAD_DIM, row0:row0 + R] = vt[hh * HEAD_DIM:(hh + 1) * HEAD_DIM, :]
                vt_ref[0, r0 + HEAD_DIM:r0 + VT_ROWS, row0:row0 + R] = ones_rows

        mxu_pieces = [(2 * PROJ_COL_CHUNK, functools.partial(project_cols, c0))
                      for c0 in range(0, nxt.shape[1], PROJ_COL_CHUNK)]
        vpu_pieces = []
        for c in range(R // SUB):
            vpu_pieces += [(760, functools.partial(lru_front, c)), (530, functools.partial(lru_back, c))]
        for pair in range(N_HEAD_PAIRS):
            vpu_pieces += [(80, functools.partial(q_epilogue, pair)),
                           (110, functools.partial(k_epilogue, pair)),
                           (40, functools.partial(v_epilogue, pair))]
        interleave(mxu_pieces, vpu_pieces)
    hcar[...] = lru_state["carry"]


def _proj_lru(x, w_in, qx, kx, conv_w, conv_b, wrg, b_rg, wig, b_ig, lam):
    B, S, D = x.shape
    C = LRU_WIDTH
    nb = S // MOBA_BLOCK
    R = PROJ_ROWS
    step_rows = PROJ_TILES_PER_STEP * R
    n_tiles = B * S // R
    steps_per_seq = S // step_rows
    assert S % step_rows == 0 and PROJ_TILES_PER_STEP == 2
    x2 = x.reshape(B * S, D)
    resident = lambda shape: pl.BlockSpec(shape, lambda i: (0,) * len(shape),
                                          pipeline_mode=pl.Buffered(1))
    tile = lambda offset: pl.BlockSpec(
        (R, D), lambda i: (jnp.minimum(PROJ_TILES_PER_STEP * i + offset, n_tiles - 1), 0))
    q, k, vt, km, lru = pl.pallas_call(
        functools.partial(_proj_kernel, steps_per_seq=steps_per_seq),
        grid=(n_tiles // PROJ_TILES_PER_STEP,),
        in_specs=[
            pl.BlockSpec((R, D), lambda i: (0, 0), pipeline_mode=pl.Buffered(1)),
            tile(1),
            tile(2),
            resident((D, 3 * ATTN_WIDTH + 2 * C)),
            resident((1, LANES)),
            pl.BlockSpec((step_rows, AUG_WIDTH), lambda i: (i % steps_per_seq, 0)),
            resident((CONV_WIDTH, C)),
            resident((1, C)),
            resident((2, C // 2, C // 2)),
            resident((1, C)),
            resident((2, C // 2, C // 2)),
            resident((1, C)),
            resident((1, C)),
        ],
        out_specs=[
            pl.BlockSpec((step_rows, AUG_WIDTH), lambda i: (i, 0)),
            pl.BlockSpec((step_rows, AUG_WIDTH), lambda i: (i, 0)),
            pl.BlockSpec((1, N_HEADS * VT_ROWS, step_rows),
                         lambda i: (i // steps_per_seq, 0, i % steps_per_seq)),
            pl.BlockSpec((1, step_rows // MOBA_BLOCK, 1, AUG_WIDTH),
                         lambda i: (i // steps_per_seq, i % steps_per_seq, 0, 0)),
            pl.BlockSpec((step_rows, C), lambda i: (i, 0)),
        ],
        out_shape=[
            jax.ShapeDtypeStruct((B * S, AUG_WIDTH), _BF16),
            jax.ShapeDtypeStruct((B * S, AUG_WIDTH), _BF16),
            jax.ShapeDtypeStruct((B, N_HEADS * VT_ROWS, S), _BF16),
            jax.ShapeDtypeStruct((B, nb, 1, AUG_WIDTH), _F32),
            jax.ShapeDtypeStruct((B * S, C), _BF16),
        ],
        scratch_shapes=[
            pltpu.VMEM((R, 3 * ATTN_WIDTH + 2 * C), _F32),
            pltpu.VMEM((R, 3 * ATTN_WIDTH + 2 * C), _F32),
            pltpu.VMEM((R + SUBLANES, C), _F32),
            pltpu.VMEM((SUBLANES, C), _F32),
        ],
        compiler_params=pltpu.CompilerParams(
            dimension_semantics=("arbitrary",),
            vmem_limit_bytes=VMEM_LIMIT_BYTES),
        name="proj_lru",
    )(x2, x2, x2, w_in, qx, kx, conv_w, conv_b, wrg, b_rg, wig, b_ig, lam)
    return (q.reshape(B, S, AUG_WIDTH), k.reshape(B, S, AUG_WIDTH), vt, km, lru)


def _aug_constants(seq_len):
    nb = seq_len // MOBA_BLOCK
    alibi_lane0 = SEL_LANE0 + nb
    lane = np.arange(LANES)
    qx = ((lane >= alibi_lane0) & (lane < alibi_lane0 + ALIBI_PIECES)).astype(np.float32).reshape(1, LANES)

    slopes = np.exp2(-8.0 * np.arange(1, N_HEADS + 1, dtype=np.float32) / N_HEADS).astype(np.float32)
    pos = np.arange(seq_len)
    onehot = (pos[:, None] // MOBA_BLOCK == np.arange(nb)[None, :]).astype(np.float32)
    val = (np.float32(LOG2E) * slopes)[None, :] * (pos[:, None] - seq_len // 2).astype(np.float32)

    def trunc_bf16(v):
        return (v.view(np.uint32) & np.uint32(0xFFFF0000)).view(np.float32)

    hi = trunc_bf16(val)
    mid = trunc_bf16(val - hi)
    lo = val - hi - mid
    pieces = np.stack([hi, mid, lo], axis=-1)
    extras = np.concatenate([
        np.zeros((seq_len, N_HEADS, HEAD_DIM), np.float32),
        np.broadcast_to(onehot[:, None, :], (seq_len, N_HEADS, nb)),
        pieces,
        np.zeros((seq_len, N_HEADS, LANES - alibi_lane0 - ALIBI_PIECES), np.float32)], axis=-1)
    return jnp.asarray(qx), jnp.asarray(extras.reshape(seq_len, AUG_WIDTH).astype(_BF16))


def _attn_kernel(q_ref, k_ref, vt_ref, km_ref, o_ref):
    blk = MOBA_BLOCK
    nb = k_ref.shape[1] // blk

    lane = lax.broadcasted_iota(jnp.int32, (blk, LANES), 1)
    sel_lane = (lane >= SEL_LANE0) & (lane < SEL_LANE0 + nb)
    key_row = lax.broadcasted_iota(jnp.int32, (blk, blk), 0)
    qry_col = lax.broadcasted_iota(jnp.int32, (blk, blk), 1)
    causal = key_row <= qry_col
    blk_idx = lax.broadcasted_iota(jnp.int32, (nb, blk), 0)

    def query_tile(j, hh):
        nk = (j + 1) * blk
        cols = slice(hh * LANES, (hh + 1) * LANES)
        qa = q_ref[0, j * blk:nk, cols]
        if j > MOBA_TOPK:
            kmh = km_ref[0, :, 0, cols].astype(_BF16)
            gate = lax.dot_general(kmh, qa, _NT_DIMS, preferred_element_type=_F32)
            rank = jnp.zeros((nb, blk), jnp.int32)
            for n2 in range(j):
                g2 = gate[n2:n2 + 1, :]
                beats = (g2 > gate) | ((g2 == gate) & (n2 < blk_idx))
                rank = rank + beats.astype(jnp.int32)
            keep = ((blk_idx < j) & (rank < MOBA_TOPK)) | (blk_idx >= j)
            selb = jnp.where(keep, 0.0, NEG_INF)
            selb_rows = jnp.concatenate(
                [jnp.zeros((SEL_LANE0, blk), _F32), selb,
                 jnp.zeros((LANES - SEL_LANE0 - nb, blk), _F32)], axis=0)
            qa = jnp.where(sel_lane, selb_rows.T.astype(_BF16), qa)
        return qa

    def scores(j, hh):
        nk = (j + 1) * blk
        cols = slice(hh * LANES, (hh + 1) * LANES)
        z = lax.dot_general(k_ref[0, 0:nk, cols], query_tile(j, hh), _NT_DIMS,
                            preferred_element_type=_F32)
        z_own = jnp.where(causal, z[nk - blk:, :], NEG_INF)
        z = z_own if j == 0 else jnp.concatenate([z[:nk - blk, :], z_own], axis=0)
        m = jnp.max(z, axis=0, keepdims=True)
        return z, m

    def probs(zm):
        z, m = zm
        return jnp.exp2(z - m).astype(_BF16)

    def weighted_values(j, hh, p):
        nk = (j + 1) * blk
        vt = vt_ref[0, hh * VT_ROWS:(hh + 1) * VT_ROWS, 0:nk]
        acc = jnp.dot(vt, p, preferred_element_type=_F32)
        return acc[0:HEAD_DIM, :] / acc[HEAD_DIM:HEAD_DIM + 1, :]

    items = [(j, hh) for j in range(nb) for hh in range(HEADS_PER_LANE_TILE)]
    n_items = len(items)
    zm, ps, outs = {}, {}, {}
    for t in range(-2 * ATTN_STAGE_LAG, n_items):
        if t + 2 * ATTN_STAGE_LAG < n_items:
            zm[t + 2 * ATTN_STAGE_LAG] = scores(*items[t + 2 * ATTN_STAGE_LAG])
        if 0 <= t + ATTN_STAGE_LAG < n_items:
            ps[t + ATTN_STAGE_LAG] = probs(zm.pop(t + ATTN_STAGE_LAG))
        if t >= 0:
            j, hh = items[t]
            outs[hh] = weighted_values(j, hh, ps.pop(t))
            if hh == HEADS_PER_LANE_TILE - 1:
                o_ref[0, j * blk:(j + 1) * blk, :] = jnp.concatenate(
                    [outs[h] for h in range(HEADS_PER_LANE_TILE)], axis=0).T.astype(o_ref.dtype)


def _moba_attention(q, k, vt, km):
    B, S, _ = q.shape
    nb = S // MOBA_BLOCK
    pair_w = HEADS_PER_LANE_TILE * LANES
    return pl.pallas_call(
        _attn_kernel,
        grid=(B, N_HEAD_PAIRS),
        in_specs=[
            pl.BlockSpec((1, S, pair_w), lambda b, h: (b, 0, h)),
            pl.BlockSpec((1, S, pair_w), lambda b, h: (b, 0, h)),
            pl.BlockSpec((1, HEADS_PER_LANE_TILE * VT_ROWS, S), lambda b, h: (b, h, 0)),
            pl.BlockSpec((1, nb, 1, pair_w), lambda b, h: (b, 0, 0, h)),
        ],
        out_specs=pl.BlockSpec((1, S, LANES), lambda b, h: (b, 0, h)),
        out_shape=jax.ShapeDtypeStruct((B, S, ATTN_WIDTH), _BF16),
        compiler_params=pltpu.CompilerParams(
            dimension_semantics=("parallel", "parallel"),
            vmem_limit_bytes=VMEM_LIMIT_BYTES),
        name="moba_attention",
    )(q, k, vt, km)


def _layer_norm(y, g, b):
    mu = jnp.mean(y, axis=-1, keepdims=True)
    yc = y - mu
    var = jnp.mean(yc * yc, axis=-1, keepdims=True)
    return yc * lax.rsqrt(var + LN_EPS) * g + b


def _mlp_kernel(x_ref, attn_ref, lru_ref, wo_ref, g1_ref, b1_ref, wup_ref, bup_ref,
                wdn_ref, bdn_ref, g2_ref, b2_ref, o_ref):
    n_sub = MLP_ROWS // MLP_SUB_ROWS
    n_ff = wup_ref.shape[1] // MLP_FF_TILE

    def rows(c):
        return slice(c * MLP_SUB_ROWS, (c + 1) * MLP_SUB_ROWS)

    def out_proj(c):
        mix = jnp.dot(attn_ref[rows(c), :], wo_ref[0:ATTN_WIDTH, :], preferred_element_type=_F32)
        return mix + jnp.dot(lru_ref[rows(c), :], wo_ref[ATTN_WIDTH:, :], preferred_element_type=_F32)

    def norm1(c, mix):
        x1 = _layer_norm(DEEPNORM_ALPHA * x_ref[rows(c), :] + mix, g1_ref[...], b1_ref[...])
        return x1, x1.astype(_BF16)

    def ff_tile(x1b, f):
        cols = slice(f * MLP_FF_TILE, (f + 1) * MLP_FF_TILE)
        h = jnp.dot(x1b, wup_ref[:, cols], preferred_element_type=_F32) + bup_ref[:, cols]
        h = jnp.maximum(h, 0.0)
        return jnp.dot((h * h).astype(_BF16), wdn_ref[cols, :], preferred_element_type=_F32)

    def norm2(c, x1, acc):
        y = DEEPNORM_ALPHA * x1 + acc + bdn_ref[...]
        o_ref[rows(c), :] = _layer_norm(y, g2_ref[...], b2_ref[...])

    x1, x1b = norm1(0, out_proj(0))
    pending = None
    for c in range(n_sub):
        nxt_mix = out_proj(c + 1) if c + 1 < n_sub else None
        nxt = None
        acc = None
        for f in range(n_ff):
            part = ff_tile(x1b, f)
            acc = part if acc is None else acc + part
            if f == 1 and nxt_mix is not None:
                nxt = norm1(c + 1, nxt_mix)
            if f == n_ff // 2 and pending is not None:
                norm2(*pending)
                pending = None
        pending = (c, x1, acc)
        if nxt is not None:
            x1, x1b = nxt
    norm2(*pending)


def _outproj_mlp(x2, attn2, lru2, w_out, g1, b1, w_up, b_up, w_down, b_down, g2, b2):
    M, D = x2.shape
    F = w_up.shape[1]
    tm = MLP_ROWS
    resident = lambda shape: pl.BlockSpec(shape, lambda i: (0, 0), pipeline_mode=pl.Buffered(1))
    return pl.pallas_call(
        _mlp_kernel,
        grid=(M // tm,),
        in_specs=[
            pl.BlockSpec((tm, D), lambda i: (i, 0)),
            pl.BlockSpec((tm, ATTN_WIDTH), lambda i: (i, 0)),
            pl.BlockSpec((tm, LRU_WIDTH), lambda i: (i, 0)),
            resident((D, D)),
            resident((1, D)), resident((1, D)),
            resident((D, F)),
            resident((1, F)),
            resident((F, D)),
            resident((1, D)), resident((1, D)), resident((1, D)),
        ],
        out_specs=pl.BlockSpec((tm, D), lambda i: (i, 0)),
        out_shape=jax.ShapeDtypeStruct((M, D), _F32),
        compiler_params=pltpu.CompilerParams(
            dimension_semantics=("parallel",),
            vmem_limit_bytes=VMEM_LIMIT_BYTES),
        name="outproj_mlp",
    )(x2, attn2, lru2, w_out, g1, b1, w_up, b_up, w_down, b_down, g2, b2)


def _block_diag_halves(w):
    n_half = N_LRU_BLOCKS // 2
    eye = jnp.eye(n_half, dtype=w.dtype)
    wh = w.reshape(2, n_half, LRU_BLOCK, LRU_BLOCK)
    full = jnp.einsum('snde,nm->sndme', wh, eye)
    return full.reshape(2, n_half * LRU_BLOCK, n_half * LRU_BLOCK)


def kernel(x, w_in, conv_w, conv_b, w_rg, b_rg, w_ig, b_ig, lru_lambda, w_out, ln1_g, ln1_b,
           w_up, b_up, w_down, b_down, ln2_g, ln2_b):
    B, S, D = x.shape
    depth = w_in.shape[0]
    row = lambda v: v.reshape(1, -1)
    qx, kx = _aug_constants(S)
    for l in range(depth):
        q, k, vt, km, lru = _proj_lru(
            x, w_in[l].astype(_BF16), qx, kx, conv_w[l], row(conv_b[l]),
            _block_diag_halves(w_rg[l]).astype(_BF16), row(b_rg[l]),
            _block_diag_halves(w_ig[l]).astype(_BF16), row(b_ig[l]),
            row(lru_lambda[l]))
        attn = _moba_attention(q, k, vt, km)
        out = _outproj_mlp(
            x.reshape(B * S, D), attn.reshape(B * S, ATTN_WIDTH), lru.reshape(B * S, LRU_WIDTH),
            w_out[l].astype(_BF16), row(ln1_g[l]), row(ln1_b[l]),
            w_up[l].astype(_BF16), row(b_up[l]), w_down[l].astype(_BF16), row(b_down[l]),
            row(ln2_g[l]), row(ln2_b[l]))
        x = out.reshape(B, S, D)
    return x
```

```python
import functools

import numpy as np

import jax
import jax.numpy as jnp
from jax import lax
from jax.experimental import pallas as pl
from jax.experimental.pallas import tpu as pltpu

D_MODEL = 1024
ATTN_WIDTH = 512
N_HEADS = 8
HEAD_DIM = 64
LRU_WIDTH = 512
N_LRU_BLOCKS = 8
LRU_BLOCK = 64
CONV_WIDTH = 4
LRU_C = 8.0
MOBA_BLOCK = 256
MOBA_TOPK = 3
D_FF = 4096
LN_EPS = 1e-5
DEEPNORM_ALPHA = 2.0 ** 0.25
NEG_INF = -1e30

SUBLANES = 8
LANES = 128
HEADS_PER_LANE_TILE = LANES // HEAD_DIM
N_HEAD_PAIRS = N_HEADS // HEADS_PER_LANE_TILE
AUG_WIDTH = N_HEADS * LANES
SEL_LANE0 = HEAD_DIM
ALIBI_PIECES = 3
VT_ROWS = HEAD_DIM + 16
LOG2E = 1.4426950408889634

ATTN_STAGE_LAG = 2
PROJ_ROWS = 512
PROJ_TILES_PER_STEP = 2
PROJ_COL_CHUNK = 512
PROJ_SUB_ROWS = 256
MLP_ROWS = 512
MLP_SUB_ROWS = 256
MLP_FF_TILE = 1024
VMEM_LIMIT_BYTES = 56 * 1024 * 1024

_F32 = jnp.float32
_BF16 = jnp.bfloat16
_NT_DIMS = (((1,), (1,)), ((), ()))


def _proj_kernel(x0_ref, xa_ref, xb_ref, w_ref, qx_ref, kx_ref, cw_ref, cb_ref, wrg_ref, brg_ref,
                 wig_ref, big_ref, lam_ref, q_ref, k_ref, vt_ref, km_ref, lru_ref,
                 p_s0, p_s1, xbuf, hcar, *, steps_per_seq):
    i = pl.program_id(0)
    R = PROJ_ROWS
    C = LRU_WIDTH
    SUB = PROJ_SUB_ROWS
    PAD = SUBLANES
    Q0, K0, V0, XR0, GR0 = 0, ATTN_WIDTH, 2 * ATTN_WIDTH, 3 * ATTN_WIDTH, 3 * ATTN_WIDTH + C

    slots = (p_s0, p_s1)

    def project(x_tile_ref, slot_ref, c0, width):
        xt = x_tile_ref[...].astype(_BF16)
        slot_ref[:, c0:c0 + width] = jnp.dot(xt, w_ref[:, c0:c0 + width],
                                             preferred_element_type=_F32)

    @pl.when(i == 0)
    def _():
        project(x0_ref, p_s0, 0, p_s0.shape[1])

    @pl.when(i % steps_per_seq == 0)
    def _():
        xbuf[0:PAD, :] = jnp.zeros((PAD, C), _F32)
        hcar[...] = jnp.zeros_like(hcar)

    data_lane = lax.broadcasted_iota(jnp.int32, (R, LANES), 1) < HEAD_DIM
    row = lax.broadcasted_iota(jnp.int32, (SUBLANES, C), 0)
    ones_rows = (lax.broadcasted_iota(jnp.int32, (VT_ROWS - HEAD_DIM, R), 0) == 0).astype(_BF16)
    half = C // 2

    def sigmoid(v):
        return 0.5 * jnp.tanh(0.5 * v) + 0.5

    def interleave(mxu_pieces, vpu_pieces):
        done = [0, 0]
        queues = [list(mxu_pieces), list(vpu_pieces)]
        while queues[0] or queues[1]:
            s = 0 if (queues[0] and (not queues[1] or done[0] <= done[1])) else 1
            cost, thunk = queues[s].pop(0)
            done[s] += cost
            thunk()

    lru_state = {"carry": hcar[...]}
    for ph, x_next_ref in enumerate((xa_ref, xb_ref)):
        cur, nxt = slots[ph], slots[1 - ph]
        row0 = ph * R
        xt = x_next_ref[...].astype(_BF16)

        def project_cols(c0, xt=xt, nxt=nxt):
            nxt[:, c0:c0 + PROJ_COL_CHUNK] = jnp.dot(xt, w_ref[:, c0:c0 + PROJ_COL_CHUNK],
                                                     preferred_element_type=_F32)

        def lru_front(c, cur=cur):
            xr = cur[c * SUB:(c + 1) * SUB, XR0:XR0 + C]
            xbuf[PAD + c * SUB:PAD + (c + 1) * SUB, :] = xr
            y = cb_ref[...] + cw_ref[CONV_WIDTH - 1:CONV_WIDTH, :] * xr
            for d in range(1, CONV_WIDTH):
                y = y + (cw_ref[CONV_WIDTH - 1 - d:CONV_WIDTH - d, :]
                         * xbuf[PAD - d + c * SUB:PAD - d + (c + 1) * SUB, :])
            if c == R // SUB - 1:
                xbuf[0:PAD, :] = xr[SUB - PAD:SUB, :]
            yb = y.astype(_BF16)

            def gate_lin(w3_ref, b_ref):
                parts = [jnp.dot(yb[:, s * half:(s + 1) * half], w3_ref[s],
                                 preferred_element_type=_F32) for s in range(2)]
                return jnp.concatenate(parts, axis=1) + b_ref[...]

            r = sigmoid(gate_lin(wrg_ref, brg_ref))
            ig = sigmoid(gate_lin(wig_ref, big_ref))
            lam = lam_ref[...]
            softplus_neg_lam = jnp.maximum(-lam, 0.0) + jnp.log(1.0 + jnp.exp(-jnp.abs(lam)))
            a = jnp.exp(r * ((-LRU_C) * softplus_neg_lam))
            v1 = 1.0 - a * a
            lru_state[c] = (a, jnp.where(v1 > 0.0, v1 * lax.rsqrt(v1), 0.0) * (ig * y))

        def lru_back(c, cur=cur, row0=row0):
            a, u = lru_state.pop(c)
            carry = lru_state["carry"]
            h_groups = []
            for g in range(SUB // SUBLANES):
                ag = a[g * SUBLANES:(g + 1) * SUBLANES, :]
                ug = u[g * SUBLANES:(g + 1) * SUBLANES, :]
                for d in (1, 2, 4):
                    a_sh = pltpu.roll(ag, d, 0)
                    u_sh = pltpu.roll(ug, d, 0)
                    m = row >= d
                    ug = jnp.where(m, ag * u_sh + ug, ug)
                    ag = jnp.where(m, ag * a_sh, ag)
                hg = ag * carry + ug
                h_groups.append(hg)
                carry = jnp.broadcast_to(hg[SUBLANES - 1:SUBLANES, :], (SUBLANES, C))
            lru_state["carry"] = carry
            gr = cur[c * SUB:(c + 1) * SUB, GR0:GR0 + C]
            gelu = 0.5 * gr * (1.0 + jnp.tanh(0.7978845608028654 * (gr + 0.044715 * (gr * gr * gr))))
            lru_ref[row0 + c * SUB:row0 + (c + 1) * SUB, :] = (
                jnp.concatenate(h_groups, axis=0) * gelu).astype(lru_ref.dtype)

        def head_pair(c0, pair, cur=cur):
            two = cur[:, c0 + pair * LANES:c0 + (pair + 1) * LANES]
            return ((pair * HEADS_PER_LANE_TILE, two),
                    (pair * HEADS_PER_LANE_TILE + 1, pltpu.roll(two, HEAD_DIM, 1)))

        def q_epilogue(pair, head_pair=head_pair, row0=row0):
            for h, qh in head_pair(Q0, pair):
                q_ref[row0:row0 + R, h * LANES:(h + 1) * LANES] = jnp.where(
                    data_lane, qh * (HEAD_DIM ** -0.5 * LOG2E), qx_ref[...]).astype(_BF16)

        def k_epilogue(pair, head_pair=head_pair, row0=row0, ph=ph):
            for h, kh in head_pair(K0, pair):
                cols = slice(h * LANES, (h + 1) * LANES)
                k_ref[row0:row0 + R, cols] = jnp.where(data_lane, kh.astype(_BF16),
                                                       kx_ref[row0:row0 + R, cols])
                kpad = jnp.where(data_lane, kh, 0.0)
                for s in range(R // MOBA_BLOCK):
                    km_ref[0, ph * (R // MOBA_BLOCK) + s, :, cols] = jnp.mean(
                        kpad[s * MOBA_BLOCK:(s + 1) * MOBA_BLOCK], axis=0, keepdims=True)

        def v_epilogue(pair, cur=cur, row0=row0):
            vt = cur[:, V0 + pair * LANES:V0 + (pair + 1) * LANES].T.astype(_BF16)
            for hh in range(HEADS_PER_LANE_TILE):
                r0 = (pair * HEADS_PER_LANE_TILE + hh) * VT_ROWS
                vt_ref[0, r0:r0 + HEAD_DIM, row0:row0 + R] = vt[hh * HEAD_DIM:(hh + 1) * HEAD_DIM, :]
                vt_ref[0, r0 + HEAD_DIM:r0 + VT_ROWS, row0:row0 + R] = ones_rows

        mxu_pieces = [(2 * PROJ_COL_CHUNK, functools.partial(project_cols, c0))
                      for c0 in range(0, nxt.shape[1], PROJ_COL_CHUNK)]
        vpu_pieces = []
        for c in range(R // SUB):
            vpu_pieces += [(6 * SUB, functools.partial(lru_front, c)), (4 * SUB, functools.partial(lru_back, c))]
        for pair in range(N_HEAD_PAIRS):
            vpu_pieces += [(80, functools.partial(q_epilogue, pair)),
                           (110, functools.partial(k_epilogue, pair)),
                           (40, functools.partial(v_epilogue, pair))]
        interleave(mxu_pieces, vpu_pieces)
    hcar[...] = lru_state["carry"]


def _proj_lru(x, w_in, qx, kx, conv_w, conv_b, wrg, b_rg, wig, b_ig, lam):
    B, S, D = x.shape
    C = LRU_WIDTH
    nb = S // MOBA_BLOCK
    R = PROJ_ROWS
    step_rows = PROJ_TILES_PER_STEP * R
    n_tiles = B * S // R
    steps_per_seq = S // step_rows
    assert S % step_rows == 0 and PROJ_TILES_PER_STEP == 2
    x2 = x.reshape(B * S, D)
    resident = lambda shape: pl.BlockSpec(shape, lambda i: (0,) * len(shape),
                                          pipeline_mode=pl.Buffered(1))
    tile = lambda offset: pl.BlockSpec(
        (R, D), lambda i: (jnp.minimum(PROJ_TILES_PER_STEP * i + offset, n_tiles - 1), 0))
    q, k, vt, km, lru = pl.pallas_call(
        functools.partial(_proj_kernel, steps_per_seq=steps_per_seq),
        grid=(n_tiles // PROJ_TILES_PER_STEP,),
        in_specs=[
            pl.BlockSpec((R, D), lambda i: (0, 0), pipeline_mode=pl.Buffered(1)),
            tile(1),
            tile(2),
            resident((D, 3 * ATTN_WIDTH + 2 * C)),
            resident((1, LANES)),
            pl.BlockSpec((step_rows, AUG_WIDTH), lambda i: (i % steps_per_seq, 0)),
            resident((CONV_WIDTH, C)),
            resident((1, C)),
            resident((2, C // 2, C // 2)),
            resident((1, C)),
            resident((2, C // 2, C // 2)),
            resident((1, C)),
            resident((1, C)),
        ],
        out_specs=[
            pl.BlockSpec((step_rows, AUG_WIDTH), lambda i: (i, 0)),
            pl.BlockSpec((step_rows, AUG_WIDTH), lambda i: (i, 0)),
            pl.BlockSpec((1, N_HEADS * VT_ROWS, step_rows),
                         lambda i: (i // steps_per_seq, 0, i % steps_per_seq)),
            pl.BlockSpec((1, step_rows // MOBA_BLOCK, 1, AUG_WIDTH),
                         lambda i: (i // steps_per_seq, i % steps_per_seq, 0, 0)),
            pl.BlockSpec((step_rows, C), lambda i: (i, 0)),
        ],
        out_shape=[
            jax.ShapeDtypeStruct((B * S, AUG_WIDTH), _BF16),
            jax.ShapeDtypeStruct((B * S, AUG_WIDTH), _BF16),
            jax.ShapeDtypeStruct((B, N_HEADS * VT_ROWS, S), _BF16),
            jax.ShapeDtypeStruct((B, nb, 1, AUG_WIDTH), _F32),
            jax.ShapeDtypeStruct((B * S, C), _BF16),
        ],
        scratch_shapes=[
            pltpu.VMEM((R, 3 * ATTN_WIDTH + 2 * C), _F32),
            pltpu.VMEM((R, 3 * ATTN_WIDTH + 2 * C), _F32),
            pltpu.VMEM((R + SUBLANES, C), _F32),
            pltpu.VMEM((SUBLANES, C), _F32),
        ],
        compiler_params=pltpu.CompilerParams(
            dimension_semantics=("arbitrary",),
            vmem_limit_bytes=VMEM_LIMIT_BYTES),
        name="proj_lru",
    )(x2, x2, x2, w_in, qx, kx, conv_w, conv_b, wrg, b_rg, wig, b_ig, lam)
    return (q.reshape(B, S, AUG_WIDTH), k.reshape(B, S, AUG_WIDTH), vt, km, lru)


def _aug_constants(seq_len):
    nb = seq_len // MOBA_BLOCK
    alibi_lane0 = SEL_LANE0 + nb
    lane = np.arange(LANES)
    qx = ((lane >= alibi_lane0) & (lane < alibi_lane0 + ALIBI_PIECES)).astype(np.float32).reshape(1, LANES)

    slopes = np.exp2(-8.0 * np.arange(1, N_HEADS + 1, dtype=np.float32) / N_HEADS).astype(np.float32)
    pos = np.arange(seq_len)
    onehot = (pos[:, None] // MOBA_BLOCK == np.arange(nb)[None, :]).astype(np.float32)
    val = (np.float32(LOG2E) * slopes)[None, :] * (pos[:, None] - seq_len // 2).astype(np.float32)

    def trunc_bf16(v):
        return (v.view(np.uint32) & np.uint32(0xFFFF0000)).view(np.float32)

    hi = trunc_bf16(val)
    mid = trunc_bf16(val - hi)
    lo = val - hi - mid
    pieces = np.stack([hi, mid, lo], axis=-1)
    extras = np.concatenate([
        np.zeros((seq_len, N_HEADS, HEAD_DIM), np.float32),
        np.broadcast_to(onehot[:, None, :], (seq_len, N_HEADS, nb)),
        pieces,
        np.zeros((seq_len, N_HEADS, LANES - alibi_lane0 - ALIBI_PIECES), np.float32)], axis=-1)
    return jnp.asarray(qx), jnp.asarray(extras.reshape(seq_len, AUG_WIDTH).astype(_BF16))


def _attn_kernel(q_ref, k_ref, vt_ref, km_ref, o_ref):
    blk = MOBA_BLOCK
    nb = k_ref.shape[1] // blk

    lane = lax.broadcasted_iota(jnp.int32, (blk, LANES), 1)
    sel_lane = (lane >= SEL_LANE0) & (lane < SEL_LANE0 + nb)
    key_row = lax.broadcasted_iota(jnp.int32, (blk, blk), 0)
    qry_col = lax.broadcasted_iota(jnp.int32, (blk, blk), 1)
    causal = key_row <= qry_col
    blk_idx = lax.broadcasted_iota(jnp.int32, (nb, blk), 0)

    def query_tile(j, hh):
        nk = (j + 1) * blk
        cols = slice(hh * LANES, (hh + 1) * LANES)
        qa = q_ref[0, j * blk:nk, cols]
        if j > MOBA_TOPK:
            kmh = km_ref[0, :, 0, cols].astype(_BF16)
            gate = lax.dot_general(kmh, qa, _NT_DIMS, preferred_element_type=_F32)
            rank = jnp.zeros((nb, blk), jnp.int32)
            for n2 in range(j):
                g2 = gate[n2:n2 + 1, :]
                beats = (g2 > gate) | ((g2 == gate) & (n2 < blk_idx))
                rank = rank + beats.astype(jnp.int32)
            keep = ((blk_idx < j) & (rank < MOBA_TOPK)) | (blk_idx >= j)
            selb = jnp.where(keep, 0.0, NEG_INF)
            selb_rows = jnp.concatenate(
                [jnp.zeros((SEL_LANE0, blk), _F32), selb,
                 jnp.zeros((LANES - SEL_LANE0 - nb, blk), _F32)], axis=0)
            qa = jnp.where(sel_lane, selb_rows.T.astype(_BF16), qa)
        return qa

    def scores(j, hh):
        nk = (j + 1) * blk
        cols = slice(hh * LANES, (hh + 1) * LANES)
        z = lax.dot_general(k_ref[0, 0:nk, cols], query_tile(j, hh), _NT_DIMS,
                            preferred_element_type=_F32)
        z_own = jnp.where(causal, z[nk - blk:, :], NEG_INF)
        z = z_own if j == 0 else jnp.concatenate([z[:nk - blk, :], z_own], axis=0)
        m = jnp.max(z, axis=0, keepdims=True)
        return z, m

    def probs(zm):
        z, m = zm
        return jnp.exp2(z - m).astype(_BF16)

    def weighted_values(j, hh, p):
        nk = (j + 1) * blk
        vt = vt_ref[0, hh * VT_ROWS:(hh + 1) * VT_ROWS, 0:nk]
        acc = jnp.dot(vt, p, preferred_element_type=_F32)
        return acc[0:HEAD_DIM, :] / acc[HEAD_DIM:HEAD_DIM + 1, :]

    items = [(j, hh) for j in range(nb) for hh in range(HEADS_PER_LANE_TILE)]
    n_items = len(items)
    zm, ps, outs = {}, {}, {}
    for t in range(-2 * ATTN_STAGE_LAG, n_items):
        if t + 2 * ATTN_STAGE_LAG < n_items:
            zm[t + 2 * ATTN_STAGE_LAG] = scores(*items[t + 2 * ATTN_STAGE_LAG])
        if 0 <= t + ATTN_STAGE_LAG < n_items:
            ps[t + ATTN_STAGE_LAG] = probs(zm.pop(t + ATTN_STAGE_LAG))
        if t >= 0:
            j, hh = items[t]
            outs[hh] = weighted_values(j, hh, ps.pop(t))
            if hh == HEADS_PER_LANE_TILE - 1:
                o_ref[0, j * blk:(j + 1) * blk, :] = jnp.concatenate(
                    [outs[h] for h in range(HEADS_PER_LANE_TILE)], axis=0).T.astype(o_ref.dtype)


def _moba_attention(q, k, vt, km):
    B, S, _ = q.shape
    nb = S // MOBA_BLOCK
    pair_w = HEADS_PER_LANE_TILE * LANES
    return pl.pallas_call(
        _attn_kernel,
        grid=(B, N_HEAD_PAIRS),
        in_specs=[
            pl.BlockSpec((1, S, pair_w), lambda b, h: (b, 0, h)),
            pl.BlockSpec((1, S, pair_w), lambda b, h: (b, 0, h)),
            pl.BlockSpec((1, HEADS_PER_LANE_TILE * VT_ROWS, S), lambda b, h: (b, h, 0)),
            pl.BlockSpec((1, nb, 1, pair_w), lambda b, h: (b, 0, 0, h)),
        ],
        out_specs=pl.BlockSpec((1, S, LANES), lambda b, h: (b, 0, h)),
        out_shape=jax.ShapeDtypeStruct((B, S, ATTN_WIDTH), _BF16),
        compiler_params=pltpu.CompilerParams(
            dimension_semantics=("parallel", "parallel"),
            vmem_limit_bytes=VMEM_LIMIT_BYTES),
        name="moba_attention",
    )(q, k, vt, km)


def _layer_norm(y, g, b):
    mu = jnp.mean(y, axis=-1, keepdims=True)
    yc = y - mu
    var = jnp.mean(yc * yc, axis=-1, keepdims=True)
    return yc * lax.rsqrt(var + LN_EPS) * g + b


def _mlp_kernel(x_ref, attn_ref, lru_ref, wo_ref, g1_ref, b1_ref, wup_ref, bup_ref,
                wdn_ref, bdn_ref, g2_ref, b2_ref, o_ref):
    n_sub = MLP_ROWS // MLP_SUB_ROWS
    n_ff = wup_ref.shape[1] // MLP_FF_TILE

    def rows(c):
        return slice(c * MLP_SUB_ROWS, (c + 1) * MLP_SUB_ROWS)

    def out_proj(c):
        mix = jnp.dot(attn_ref[rows(c), :], wo_ref[0:ATTN_WIDTH, :], preferred_element_type=_F32)
        return mix + jnp.dot(lru_ref[rows(c), :], wo_ref[ATTN_WIDTH:, :], preferred_element_type=_F32)

    def norm1(c, mix):
        x1 = _layer_norm(DEEPNORM_ALPHA * x_ref[rows(c), :] + mix, g1_ref[...], b1_ref[...])
        return x1, x1.astype(_BF16)

    def ff_tile(x1b, f):
        cols = slice(f * MLP_FF_TILE, (f + 1) * MLP_FF_TILE)
        h = jnp.dot(x1b, wup_ref[:, cols], preferred_element_type=_F32) + bup_ref[:, cols]
        h = jnp.maximum(h, 0.0)
        return jnp.dot((h * h).astype(_BF16), wdn_ref[cols, :], preferred_element_type=_F32)

    def norm2(c, x1, acc):
        y = DEEPNORM_ALPHA * x1 + acc + bdn_ref[...]
        o_ref[rows(c), :] = _layer_norm(y, g2_ref[...], b2_ref[...])

    x1, x1b = norm1(0, out_proj(0))
    pending = None
    for c in range(n_sub):
        nxt_mix = out_proj(c + 1) if c + 1 < n_sub else None
        nxt = None
        acc = None
        for f in range(n_ff):
            part = ff_tile(x1b, f)
            acc = part if acc is None else acc + part
            if f == 1 and nxt_mix is not None:
                nxt = norm1(c + 1, nxt_mix)
            if f == n_ff // 2 and pending is not None:
                norm2(*pending)
                pending = None
        pending = (c, x1, acc)
        if nxt is not None:
            x1, x1b = nxt
    norm2(*pending)


def _outproj_mlp(x2, attn2, lru2, w_out, g1, b1, w_up, b_up, w_down, b_down, g2, b2):
    M, D = x2.shape
    F = w_up.shape[1]
    tm = MLP_ROWS
    resident = lambda shape: pl.BlockSpec(shape, lambda i: (0, 0), pipeline_mode=pl.Buffered(1))
    return pl.pallas_call(
        _mlp_kernel,
        grid=(M // tm,),
        in_specs=[
            pl.BlockSpec((tm, D), lambda i: (i, 0)),
            pl.BlockSpec((tm, ATTN_WIDTH), lambda i: (i, 0)),
            pl.BlockSpec((tm, LRU_WIDTH), lambda i: (i, 0)),
            resident((D, D)),
            resident((1, D)), resident((1, D)),
            resident((D, F)),
            resident((1, F)),
            resident((F, D)),
            resident((1, D)), resident((1, D)), resident((1, D)),
        ],
        out_specs=pl.BlockSpec((tm, D), lambda i: (i, 0)),
        out_shape=jax.ShapeDtypeStruct((M, D), _F32),
        compiler_params=pltpu.CompilerParams(
            dimension_semantics=("parallel",),
            vmem_limit_bytes=VMEM_LIMIT_BYTES),
        name="outproj_mlp",
    )(x2, attn2, lru2, w_out, g1, b1, w_up, b_up, w_down, b_down, g2, b2)


def _block_diag_halves(w):
    n_half = N_LRU_BLOCKS // 2
    eye = jnp.eye(n_half, dtype=w.dtype)
    wh = w.reshape(2, n_half, LRU_BLOCK, LRU_BLOCK)
    full = jnp.einsum('snde,nm->sndme', wh, eye)
    return full.reshape(2, n_half * LRU_BLOCK, n_half * LRU_BLOCK)


def kernel(x, w_in, conv_w, conv_b, w_rg, b_rg, w_ig, b_ig, lru_lambda, w_out, ln1_g, ln1_b,
           w_up, b_up, w_down, b_down, ln2_g, ln2_b):
    B, S, D = x.shape
    depth = w_in.shape[0]
    row = lambda v: v.reshape(1, -1)
    qx, kx = _aug_constants(S)
    for l in range(depth):
        q, k, vt, km, lru = _proj_lru(
            x, w_in[l].astype(_BF16), qx, kx, conv_w[l], row(conv_b[l]),
            _block_diag_halves(w_rg[l]).astype(_BF16), row(b_rg[l]),
            _block_diag_halves(w_ig[l]).astype(_BF16), row(b_ig[l]),
            row(lru_lambda[l]))
        attn = _moba_attention(q, k, vt, km)
        out = _outproj_mlp(
            x.reshape(B * S, D), attn.reshape(B * S, ATTN_WIDTH), lru.reshape(B * S, LRU_WIDTH),
            w_out[l].astype(_BF16), row(ln1_g[l]), row(ln1_b[l]),
            w_up[l].astype(_BF16), row(b_up[l]), w_down[l].astype(_BF16), row(b_down[l]),
            row(ln2_g[l]), row(ln2_b[l]))
        x = out.reshape(B, S, D)
    return x
```

```python
import functools

import numpy as np

import jax
import jax.numpy as jnp
from jax import lax
from jax.experimental import pallas as pl
from jax.experimental.pallas import tpu as pltpu

D_MODEL = 1024
ATTN_WIDTH = 512
N_HEADS = 8
HEAD_DIM = 64
LRU_WIDTH = 512
N_LRU_BLOCKS = 8
LRU_BLOCK = 64
CONV_WIDTH = 4
LRU_C = 8.0
MOBA_BLOCK = 256
MOBA_TOPK = 3
D_FF = 4096
LN_EPS = 1e-5
DEEPNORM_ALPHA = 2.0 ** 0.25
NEG_INF = -1e30

SUBLANES = 8
LANES = 128
HEADS_PER_LANE_TILE = LANES // HEAD_DIM
N_HEAD_PAIRS = N_HEADS // HEADS_PER_LANE_TILE
AUG_WIDTH = N_HEADS * LANES
SEL_LANE0 = HEAD_DIM
ALIBI_PIECES = 3
VT_ROWS = HEAD_DIM + 16
LOG2E = 1.4426950408889634

ATTN_STAGE_LAG = 2
PROJ_ROWS = 512
PROJ_TILES_PER_STEP = 2
PROJ_COL_CHUNK = 512
PROJ_SUB_ROWS = 256
MLP_ROWS = 512
MLP_SUB_ROWS = 256
MLP_FF_TILE = 1024
VMEM_LIMIT_BYTES = 56 * 1024 * 1024

_F32 = jnp.float32
_BF16 = jnp.bfloat16
_NT_DIMS = (((1,), (1,)), ((), ()))


def _proj_kernel(x0_ref, xa_ref, xb_ref, w_ref, qx_ref, kx_ref, cw_ref, cb_ref, wrg_ref, brg_ref,
                 wig_ref, big_ref, lam_ref, q_ref, k_ref, vt_ref, km_ref, lru_ref,
                 p_s0, p_s1, xbuf, hcar, *, steps_per_seq):
    i = pl.program_id(0)
    R = PROJ_ROWS
    C = LRU_WIDTH
    SUB = PROJ_SUB_ROWS
    PAD = SUBLANES
    Q0, K0, V0, XR0, GR0 = 0, ATTN_WIDTH, 2 * ATTN_WIDTH, 3 * ATTN_WIDTH, 3 * ATTN_WIDTH + C

    slots = (p_s0, p_s1)

    def project(x_tile_ref, slot_ref, c0, width):
        xt = x_tile_ref[...].astype(_BF16)
        slot_ref[:, c0:c0 + width] = jnp.dot(xt, w_ref[:, c0:c0 + width],
                                             preferred_element_type=_F32)

    @pl.when(i == 0)
    def _():
        project(x0_ref, p_s0, 0, p_s0.shape[1])

    @pl.when(i % steps_per_seq == 0)
    def _():
        xbuf[0:PAD, :] = jnp.zeros((PAD, C), _F32)
        hcar[...] = jnp.zeros_like(hcar)

    data_lane = lax.broadcasted_iota(jnp.int32, (R, LANES), 1) < HEAD_DIM
    row = lax.broadcasted_iota(jnp.int32, (SUBLANES, C), 0)
    ones_rows = (lax.broadcasted_iota(jnp.int32, (VT_ROWS - HEAD_DIM, R), 0) == 0).astype(_BF16)
    half = C // 2

    def sigmoid(v):
        return 0.5 * jnp.tanh(0.5 * v) + 0.5

    def interleave(mxu_pieces, vpu_pieces):
        done = [0, 0]
        queues = [list(mxu_pieces), list(vpu_pieces)]
        while queues[0] or queues[1]:
            s = 0 if (queues[0] and (not queues[1] or done[0] <= done[1])) else 1
            cost, thunk = queues[s].pop(0)
            done[s] += cost
            thunk()

    lru_state = {"carry": hcar[...]}
    for ph, x_next_ref in enumerate((xa_ref, xb_ref)):
        cur, nxt = slots[ph], slots[1 - ph]
        row0 = ph * R
        xt = x_next_ref[...].astype(_BF16)

        def project_cols(c0, xt=xt, nxt=nxt):
            nxt[:, c0:c0 + PROJ_COL_CHUNK] = jnp.dot(xt, w_ref[:, c0:c0 + PROJ_COL_CHUNK],
                                                     preferred_element_type=_F32)

        def lru_front(c, cur=cur):
            xr = cur[c * SUB:(c + 1) * SUB, XR0:XR0 + C]
            xbuf[PAD + c * SUB:PAD + (c + 1) * SUB, :] = xr
            y = cb_ref[...] + cw_ref[CONV_WIDTH - 1:CONV_WIDTH, :] * xr
            for d in range(1, CONV_WIDTH):
                y = y + (cw_ref[CONV_WIDTH - 1 - d:CONV_WIDTH - d, :]
                         * xbuf[PAD - d + c * SUB:PAD - d + (c + 1) * SUB, :])
            if c == R // SUB - 1:
                xbuf[0:PAD, :] = xr[SUB - PAD:SUB, :]
            yb = y.astype(_BF16)

            def gate_lin(w3_ref, b_ref):
                parts = [jnp.dot(yb[:, s * half:(s + 1) * half], w3_ref[s],
                                 preferred_element_type=_F32) for s in range(2)]
                return jnp.concatenate(parts, axis=1) + b_ref[...]

            r = sigmoid(gate_lin(wrg_ref, brg_ref))
            ig = sigmoid(gate_lin(wig_ref, big_ref))
            lam = lam_ref[...]
            softplus_neg_lam = jnp.maximum(-lam, 0.0) + jnp.log(1.0 + jnp.exp(-jnp.abs(lam)))
            a = jnp.exp(r * ((-LRU_C) * softplus_neg_lam))
            v1 = 1.0 - a * a
            lru_state[c] = (a, jnp.where(v1 > 0.0, v1 * lax.rsqrt(v1), 0.0) * (ig * y))

        def lru_back(c, cur=cur, row0=row0):
            a, u = lru_state.pop(c)
            carry = lru_state["carry"]
            h_groups = []
            for g in range(SUB // SUBLANES):
                ag = a[g * SUBLANES:(g + 1) * SUBLANES, :]
                ug = u[g * SUBLANES:(g + 1) * SUBLANES, :]
                for d in (1, 2, 4):
                    a_sh = pltpu.roll(ag, d, 0)
                    u_sh = pltpu.roll(ug, d, 0)
                    m = row >= d
                    ug = jnp.where(m, ag * u_sh + ug, ug)
                    ag = jnp.where(m, ag * a_sh, ag)
                hg = ag * carry + ug
                h_groups.append(hg)
                carry = jnp.broadcast_to(hg[SUBLANES - 1:SUBLANES, :], (SUBLANES, C))
            lru_state["carry"] = carry
            gr = cur[c * SUB:(c + 1) * SUB, GR0:GR0 + C]
            gelu = 0.5 * gr * (1.0 + jnp.tanh(0.7978845608028654 * (gr + 0.044715 * (gr * gr * gr))))
            lru_ref[row0 + c * SUB:row0 + (c + 1) * SUB, :] = (
                jnp.concatenate(h_groups, axis=0) * gelu).astype(lru_ref.dtype)

        def head_pair(c0, pair, cur=cur):
            two = cur[:, c0 + pair * LANES:c0 + (pair + 1) * LANES]
            return ((pair * HEADS_PER_LANE_TILE, two),
                    (pair * HEADS_PER_LANE_TILE + 1, pltpu.roll(two, HEAD_DIM, 1)))

        def q_epilogue(pair, head_pair=head_pair, row0=row0):
            for h, qh in head_pair(Q0, pair):
                q_ref[row0:row0 + R, h * LANES:(h + 1) * LANES] = jnp.where(
                    data_lane, qh * (HEAD_DIM ** -0.5 * LOG2E), qx_ref[...]).astype(_BF16)

        def k_epilogue(pair, head_pair=head_pair, row0=row0, ph=ph):
            for h, kh in head_pair(K0, pair):
                cols = slice(h * LANES, (h + 1) * LANES)
                k_ref[row0:row0 + R, cols] = jnp.where(data_lane, kh.astype(_BF16),
                                                       kx_ref[row0:row0 + R, cols])
                kpad = jnp.where(data_lane, kh, 0.0)
                for s in range(R // MOBA_BLOCK):
                    km_ref[0, ph * (R // MOBA_BLOCK) + s, :, cols] = jnp.mean(
                        kpad[s * MOBA_BLOCK:(s + 1) * MOBA_BLOCK], axis=0, keepdims=True)

        def v_epilogue(pair, cur=cur, row0=row0):
            vt = cur[:, V0 + pair * LANES:V0 + (pair + 1) * LANES].T.astype(_BF16)
            for hh in range(HEADS_PER_LANE_TILE):
                r0 = (pair * HEADS_PER_LANE_TILE + hh) * VT_ROWS
                vt_ref[0, r0:r0 + HEAD_DIM, row0:row0 + R] = vt[hh * HEAD_DIM:(hh + 1) * HEAD_DIM, :]
                vt_ref[0, r0 + HEAD_DIM:r0 + VT_ROWS, row0:row0 + R] = ones_rows

        mxu_pieces = [(2 * PROJ_COL_CHUNK, functools.partial(project_cols, c0))
                      for c0 in range(0, nxt.shape[1], PROJ_COL_CHUNK)]
        vpu_pieces = []
        for c in range(R // SUB):
            vpu_pieces += [(6 * SUB, functools.partial(lru_front, c)), (4 * SUB, functools.partial(lru_back, c))]
        for pair in range(N_HEAD_PAIRS):
            vpu_pieces += [(80, functools.partial(q_epilogue, pair)),
                           (110, functools.partial(k_epilogue, pair)),
                           (40, functools.partial(v_epilogue, pair))]
        interleave(mxu_pieces, vpu_pieces)
    hcar[...] = lru_state["carry"]


def _proj_lru(x, w_in, qx, kx, conv_w, conv_b, wrg, b_rg, wig, b_ig, lam):
    B, S, D = x.shape
    C = LRU_WIDTH
    nb = S // MOBA_BLOCK
    R = PROJ_ROWS
    step_rows = PROJ_TILES_PER_STEP * R
    n_tiles = B * S // R
    steps_per_seq = S // step_rows
    assert S % step_rows == 0 and PROJ_TILES_PER_STEP == 2
    x2 = x.reshape(B * S, D)
    resident = lambda shape: pl.BlockSpec(shape, lambda i: (0,) * len(shape),
                                          pipeline_mode=pl.Buffered(1))
    tile = lambda offset: pl.BlockSpec(
        (R, D), lambda i: (jnp.minimum(PROJ_TILES_PER_STEP * i + offset, n_tiles - 1), 0))
    q, k, vt, km, lru = pl.pallas_call(
        functools.partial(_proj_kernel, steps_per_seq=steps_per_seq),
        grid=(n_tiles // PROJ_TILES_PER_STEP,),
        in_specs=[
            pl.BlockSpec((R, D), lambda i: (0, 0), pipeline_mode=pl.Buffered(1)),
            tile(1),
            tile(2),
            resident((D, 3 * ATTN_WIDTH + 2 * C)),
            resident((1, LANES)),
            pl.BlockSpec((step_rows, AUG_WIDTH), lambda i: (i % steps_per_seq, 0)),
            resident((CONV_WIDTH, C)),
            resident((1, C)),
            resident((2, C // 2, C // 2)),
            resident((1, C)),
            resident((2, C // 2, C // 2)),
            resident((1, C)),
            resident((1, C)),
        ],
        out_specs=[
            pl.BlockSpec((step_rows, AUG_WIDTH), lambda i: (i, 0)),
            pl.BlockSpec((step_rows, AUG_WIDTH), lambda i: (i, 0)),
            pl.BlockSpec((1, N_HEADS * VT_ROWS, step_rows),
                         lambda i: (i // steps_per_seq, 0, i % steps_per_seq)),
            pl.BlockSpec((1, step_rows // MOBA_BLOCK, 1, AUG_WIDTH),
                         lambda i: (i // steps_per_seq, i % steps_per_seq, 0, 0)),
            pl.BlockSpec((step_rows, C), lambda i: (i, 0)),
        ],
        out_shape=[
            jax.ShapeDtypeStruct((B * S, AUG_WIDTH), _BF16),
            jax.ShapeDtypeStruct((B * S, AUG_WIDTH), _BF16),
            jax.ShapeDtypeStruct((B, N_HEADS * VT_ROWS, S), _BF16),
            jax.ShapeDtypeStruct((B, nb, 1, AUG_WIDTH), _F32),
            jax.ShapeDtypeStruct((B * S, C), _BF16),
        ],
        scratch_shapes=[
            pltpu.VMEM((R, 3 * ATTN_WIDTH + 2 * C), _F32),
            pltpu.VMEM((R, 3 * ATTN_WIDTH + 2 * C), _F32),
            pltpu.VMEM((R + SUBLANES, C), _F32),
            pltpu.VMEM((SUBLANES, C), _F32),
        ],
        compiler_params=pltpu.CompilerParams(
            dimension_semantics=("arbitrary",),
            vmem_limit_bytes=VMEM_LIMIT_BYTES),
        name="proj_lru",
    )(x2, x2, x2, w_in, qx, kx, conv_w, conv_b, wrg, b_rg, wig, b_ig, lam)
    return (q.reshape(B, S, AUG_WIDTH), k.reshape(B, S, AUG_WIDTH), vt, km, lru)


def _aug_constants(seq_len):
    nb = seq_len // MOBA_BLOCK
    alibi_lane0 = SEL_LANE0 + nb
    lane = np.arange(LANES)
    qx = ((lane >= alibi_lane0) & (lane < alibi_lane0 + ALIBI_PIECES)).astype(np.float32).reshape(1, LANES)

    slopes = np.exp2(-8.0 * np.arange(1, N_HEADS + 1, dtype=np.float32) / N_HEADS).astype(np.float32)
    pos = np.arange(seq_len)
    onehot = (pos[:, None] // MOBA_BLOCK == np.arange(nb)[None, :]).astype(np.float32)
    val = (np.float32(LOG2E) * slopes)[None, :] * (pos[:, None] - seq_len // 2).astype(np.float32)

    def trunc_bf16(v):
        return (v.view(np.uint32) & np.uint32(0xFFFF0000)).view(np.float32)

    hi = trunc_bf16(val)
    mid = trunc_bf16(val - hi)
    lo = val - hi - mid
    pieces = np.stack([hi, mid, lo], axis=-1)
    extras = np.concatenate([
        np.zeros((seq_len, N_HEADS, HEAD_DIM), np.float32),
        np.broadcast_to(onehot[:, None, :], (seq_len, N_HEADS, nb)),
        pieces,
        np.zeros((seq_len, N_HEADS, LANES - alibi_lane0 - ALIBI_PIECES), np.float32)], axis=-1)
    return jnp.asarray(qx), jnp.asarray(extras.reshape(seq_len, AUG_WIDTH).astype(_BF16))


def _attn_kernel(q_ref, k_ref, vt_ref, km_ref, wo_ref, wup_ref, wdn_ref,
                 o_ref, wo_b_ref, wup_b_ref, wdn_b_ref):
    blk = MOBA_BLOCK
    nb = k_ref.shape[1] // blk

    wo_b_ref[...] = wo_ref[...].astype(_BF16)
    wup_b_ref[...] = wup_ref[...].astype(_BF16)
    wdn_b_ref[...] = wdn_ref[...].astype(_BF16)

    lane = lax.broadcasted_iota(jnp.int32, (blk, LANES), 1)
    sel_lane = (lane >= SEL_LANE0) & (lane < SEL_LANE0 + nb)
    key_row = lax.broadcasted_iota(jnp.int32, (blk, blk), 0)
    qry_col = lax.broadcasted_iota(jnp.int32, (blk, blk), 1)
    causal = key_row <= qry_col
    blk_idx = lax.broadcasted_iota(jnp.int32, (nb, blk), 0)

    def query_tile(j, hh):
        nk = (j + 1) * blk
        cols = slice(hh * LANES, (hh + 1) * LANES)
        qa = q_ref[0, j * blk:nk, cols]
        if j > MOBA_TOPK:
            kmh = km_ref[0, :, 0, cols].astype(_BF16)
            gate = lax.dot_general(kmh, qa, _NT_DIMS, preferred_element_type=_F32)
            rank = jnp.zeros((nb, blk), jnp.int32)
            for n2 in range(j):
                g2 = gate[n2:n2 + 1, :]
                beats = (g2 > gate) | ((g2 == gate) & (n2 < blk_idx))
                rank = rank + beats.astype(jnp.int32)
            keep = ((blk_idx < j) & (rank < MOBA_TOPK)) | (blk_idx >= j)
            selb = jnp.where(keep, 0.0, NEG_INF)
            selb_rows = jnp.concatenate(
                [jnp.zeros((SEL_LANE0, blk), _F32), selb,
                 jnp.zeros((LANES - SEL_LANE0 - nb, blk), _F32)], axis=0)
            qa = jnp.where(sel_lane, selb_rows.T.astype(_BF16), qa)
        return qa

    def scores(j, hh):
        nk = (j + 1) * blk
        cols = slice(hh * LANES, (hh + 1) * LANES)
        z = lax.dot_general(k_ref[0, 0:nk, cols], query_tile(j, hh), _NT_DIMS,
                            preferred_element_type=_F32)
        z_own = jnp.where(causal, z[nk - blk:, :], NEG_INF)
        z = z_own if j == 0 else jnp.concatenate([z[:nk - blk, :], z_own], axis=0)
        m = jnp.max(z, axis=0, keepdims=True)
        return z, m

    def probs(zm):
        z, m = zm
        return jnp.exp2(z - m).astype(_BF16)

    def weighted_values(j, hh, p):
        nk = (j + 1) * blk
        vt = vt_ref[0, hh * VT_ROWS:(hh + 1) * VT_ROWS, 0:nk]
        acc = jnp.dot(vt, p, preferred_element_type=_F32)
        return acc[0:HEAD_DIM, :] / acc[HEAD_DIM:HEAD_DIM + 1, :]

    items = [(j, hh) for j in range(nb) for hh in range(HEADS_PER_LANE_TILE)]
    n_items = len(items)
    zm, ps, outs = {}, {}, {}
    for t in range(-2 * ATTN_STAGE_LAG, n_items):
        if t + 2 * ATTN_STAGE_LAG < n_items:
            zm[t + 2 * ATTN_STAGE_LAG] = scores(*items[t + 2 * ATTN_STAGE_LAG])
        if 0 <= t + ATTN_STAGE_LAG < n_items:
            ps[t + ATTN_STAGE_LAG] = probs(zm.pop(t + ATTN_STAGE_LAG))
        if t >= 0:
            j, hh = items[t]
            outs[hh] = weighted_values(j, hh, ps.pop(t))
            if hh == HEADS_PER_LANE_TILE - 1:
                o_ref[0, j * blk:(j + 1) * blk, :] = jnp.concatenate(
                    [outs[h] for h in range(HEADS_PER_LANE_TILE)], axis=0).T.astype(o_ref.dtype)


def _moba_attention(q, k, vt, km, w_out, w_up, w_down):
    B, S, _ = q.shape
    nb = S // MOBA_BLOCK
    pair_w = HEADS_PER_LANE_TILE * LANES
    n_steps = B * N_HEAD_PAIRS

    def row_slab(w):
        rows = w.shape[0] // n_steps
        assert w.shape[0] % n_steps == 0 and rows % 16 == 0
        return pl.BlockSpec((rows, w.shape[1]), lambda b, h: (b * N_HEAD_PAIRS + h, 0))

    weights = (w_out, w_up, w_down)
    return pl.pallas_call(
        _attn_kernel,
        grid=(B, N_HEAD_PAIRS),
        in_specs=[
            pl.BlockSpec((1, S, pair_w), lambda b, h: (b, 0, h)),
            pl.BlockSpec((1, S, pair_w), lambda b, h: (b, 0, h)),
            pl.BlockSpec((1, HEADS_PER_LANE_TILE * VT_ROWS, S), lambda b, h: (b, h, 0)),
            pl.BlockSpec((1, nb, 1, pair_w), lambda b, h: (b, 0, 0, h)),
        ] + [row_slab(w) for w in weights],
        out_specs=[pl.BlockSpec((1, S, LANES), lambda b, h: (b, 0, h))]
        + [row_slab(w) for w in weights],
        out_shape=[jax.ShapeDtypeStruct((B, S, ATTN_WIDTH), _BF16)]
        + [jax.ShapeDtypeStruct(w.shape, _BF16) for w in weights],
        compiler_params=pltpu.CompilerParams(
            dimension_semantics=("parallel", "parallel"),
            vmem_limit_bytes=VMEM_LIMIT_BYTES),
        name="moba_attention",
    )(q, k, vt, km, *weights)


def _layer_norm(y, g, b):
    mu = jnp.mean(y, axis=-1, keepdims=True)
    yc = y - mu
    var = jnp.mean(yc * yc, axis=-1, keepdims=True)
    return yc * lax.rsqrt(var + LN_EPS) * g + b


def _mlp_kernel(x_ref, attn_ref, lru_ref, wo_ref, g1_ref, b1_ref, wup_ref, bup_ref,
                wdn_ref, bdn_ref, g2_ref, b2_ref, o_ref):
    n_sub = MLP_ROWS // MLP_SUB_ROWS
    n_ff = wup_ref.shape[1] // MLP_FF_TILE

    def rows(c):
        return slice(c * MLP_SUB_ROWS, (c + 1) * MLP_SUB_ROWS)

    def out_proj(c):
        mix = jnp.dot(attn_ref[rows(c), :], wo_ref[0:ATTN_WIDTH, :], preferred_element_type=_F32)
        return mix + jnp.dot(lru_ref[rows(c), :], wo_ref[ATTN_WIDTH:, :], preferred_element_type=_F32)

    def norm1(c, mix):
        x1 = _layer_norm(DEEPNORM_ALPHA * x_ref[rows(c), :] + mix, g1_ref[...], b1_ref[...])
        return x1, x1.astype(_BF16)

    def ff_tile(x1b, f):
        cols = slice(f * MLP_FF_TILE, (f + 1) * MLP_FF_TILE)
        h = jnp.dot(x1b, wup_ref[:, cols], preferred_element_type=_F32) + bup_ref[:, cols]
        h = jnp.maximum(h, 0.0)
        return jnp.dot((h * h).astype(_BF16), wdn_ref[cols, :], preferred_element_type=_F32)

    def norm2(c, x1, acc):
        y = DEEPNORM_ALPHA * x1 + acc + bdn_ref[...]
        o_ref[rows(c), :] = _layer_norm(y, g2_ref[...], b2_ref[...])

    x1, x1b = norm1(0, out_proj(0))
    pending = None
    for c in range(n_sub):
        nxt_mix = out_proj(c + 1) if c + 1 < n_sub else None
        nxt = None
        acc = None
        for f in range(n_ff):
            part = ff_tile(x1b, f)
            acc = part if acc is None else acc + part
            if f == 1 and nxt_mix is not None:
                nxt = norm1(c + 1, nxt_mix)
            if f == n_ff // 2 and pending is not None:
                norm2(*pending)
                pending = None
        pending = (c, x1, acc)
        if nxt is not None:
            x1, x1b = nxt
    norm2(*pending)


def _outproj_mlp(x2, attn2, lru2, w_out, g1, b1, w_up, b_up, w_down, b_down, g2, b2):
    M, D = x2.shape
    F = w_up.shape[1]
    tm = MLP_ROWS
    resident = lambda shape: pl.BlockSpec(shape, lambda i: (0, 0), pipeline_mode=pl.Buffered(1))
    return pl.pallas_call(
        _mlp_kernel,
        grid=(M // tm,),
        in_specs=[
            pl.BlockSpec((tm, D), lambda i: (i, 0)),
            pl.BlockSpec((tm, ATTN_WIDTH), lambda i: (i, 0)),
            pl.BlockSpec((tm, LRU_WIDTH), lambda i: (i, 0)),
            resident((D, D)),
            resident((1, D)), resident((1, D)),
            resident((D, F)),
            resident((1, F)),
            resident((F, D)),
            resident((1, D)), resident((1, D)), resident((1, D)),
        ],
        out_specs=pl.BlockSpec((tm, D), lambda i: (i, 0)),
        out_shape=jax.ShapeDtypeStruct((M, D), _F32),
        compiler_params=pltpu.CompilerParams(
            dimension_semantics=("parallel",),
            vmem_limit_bytes=VMEM_LIMIT_BYTES),
        name="outproj_mlp",
    )(x2, attn2, lru2, w_out, g1, b1, w_up, b_up, w_down, b_down, g2, b2)


def _block_diag_halves(w):
    n_half = N_LRU_BLOCKS // 2
    eye = jnp.eye(n_half, dtype=w.dtype)
    wh = w.reshape(2, n_half, LRU_BLOCK, LRU_BLOCK)
    full = jnp.einsum('snde,nm->sndme', wh, eye)
    return full.reshape(2, n_half * LRU_BLOCK, n_half * LRU_BLOCK)


def kernel(x, w_in, conv_w, conv_b, w_rg, b_rg, w_ig, b_ig, lru_lambda, w_out, ln1_g, ln1_b,
           w_up, b_up, w_down, b_down, ln2_g, ln2_b):
    B, S, D = x.shape
    depth = w_in.shape[0]
    row = lambda v: v.reshape(1, -1)
    qx, kx = _aug_constants(S)
    for l in range(depth):
        q, k, vt, km, lru = _proj_lru(
            x, w_in[l].astype(_BF16), qx, kx, conv_w[l], row(conv_b[l]),
            _block_diag_halves(w_rg[l]).astype(_BF16), row(b_rg[l]),
            _block_diag_halves(w_ig[l]).astype(_BF16), row(b_ig[l]),
            row(lru_lambda[l]))
        attn, w_out_b, w_up_b, w_down_b = _moba_attention(q, k, vt, km, w_out[l], w_up[l], w_down[l])
        out = _outproj_mlp(
            x.reshape(B * S, D), attn.reshape(B * S, ATTN_WIDTH), lru.reshape(B * S, LRU_WIDTH),
            w_out_b, row(ln1_g[l]), row(ln1_b[l]),
            w_up_b, row(b_up[l]), w_down_b, row(b_down[l]),
            row(ln2_g[l]), row(ln2_b[l]))
        x = out.reshape(B, S, D)
    return x
```

```python
import functools

import numpy as np

import jax
import jax.numpy as jnp
from jax import lax
from jax.experimental import pallas as pl
from jax.experimental.pallas import tpu as pltpu

D_MODEL = 1024
ATTN_WIDTH = 512
N_HEADS = 8
HEAD_DIM = 64
LRU_WIDTH = 512
N_LRU_BLOCKS = 8
LRU_BLOCK = 64
CONV_WIDTH = 4
LRU_C = 8.0
MOBA_BLOCK = 256
MOBA_TOPK = 3
D_FF = 4096
LN_EPS = 1e-5
DEEPNORM_ALPHA = 2.0 ** 0.25
NEG_INF = -1e30

SUBLANES = 8
LANES = 128
HEADS_PER_LANE_TILE = LANES // HEAD_DIM
N_HEAD_PAIRS = N_HEADS // HEADS_PER_LANE_TILE
AUG_WIDTH = N_HEADS * LANES
SEL_LANE0 = HEAD_DIM
ALIBI_PIECES = 3
VT_ROWS = HEAD_DIM + 16
LOG2E = 1.4426950408889634

ATTN_STAGE_LAG = 2
PROJ_ROWS = 512
PROJ_TILES_PER_STEP = 2
PROJ_COL_CHUNK = 512
PROJ_SUB_ROWS = 256
MLP_ROWS = 512
MLP_SUB_ROWS = 256
MLP_FF_TILE = 1024
VMEM_LIMIT_BYTES = 56 * 1024 * 1024

_F32 = jnp.float32
_BF16 = jnp.bfloat16
_NT_DIMS = (((1,), (1,)), ((), ()))


def _proj_kernel(x0_ref, xa_ref, xb_ref, w32_ref, qx_ref, kx_ref, cw_ref, cb_ref, wrg_ref, brg_ref,
                 wig_ref, big_ref, lam_ref, q_ref, k_ref, vt_ref, km_ref, lru_ref,
                 w_ref, p_s0, p_s1, xbuf, hcar, *, steps_per_seq):
    i = pl.program_id(0)
    R = PROJ_ROWS
    C = LRU_WIDTH
    SUB = PROJ_SUB_ROWS
    PAD = SUBLANES
    Q0, K0, V0, XR0, GR0 = 0, ATTN_WIDTH, 2 * ATTN_WIDTH, 3 * ATTN_WIDTH, 3 * ATTN_WIDTH + C

    slots = (p_s0, p_s1)

    def project(x_tile_ref, slot_ref, c0, width):
        xt = x_tile_ref[...].astype(_BF16)
        slot_ref[:, c0:c0 + width] = jnp.dot(xt, w_ref[:, c0:c0 + width],
                                             preferred_element_type=_F32)

    @pl.when(i == 0)
    def _():
        w_ref[...] = w32_ref[...].astype(_BF16)
        project(x0_ref, p_s0, 0, p_s0.shape[1])

    @pl.when(i % steps_per_seq == 0)
    def _():
        xbuf[0:PAD, :] = jnp.zeros((PAD, C), _F32)
        hcar[...] = jnp.zeros_like(hcar)

    data_lane = lax.broadcasted_iota(jnp.int32, (R, LANES), 1) < HEAD_DIM
    row = lax.broadcasted_iota(jnp.int32, (SUBLANES, C), 0)
    ones_rows = (lax.broadcasted_iota(jnp.int32, (VT_ROWS - HEAD_DIM, R), 0) == 0).astype(_BF16)
    half = C // 2

    def sigmoid(v):
        return 0.5 * jnp.tanh(0.5 * v) + 0.5

    def interleave(mxu_pieces, vpu_pieces):
        done = [0, 0]
        queues = [list(mxu_pieces), list(vpu_pieces)]
        while queues[0] or queues[1]:
            s = 0 if (queues[0] and (not queues[1] or done[0] <= done[1])) else 1
            cost, thunk = queues[s].pop(0)
            done[s] += cost
            thunk()

    lru_state = {"carry": hcar[...]}
    for ph, x_next_ref in enumerate((xa_ref, xb_ref)):
        cur, nxt = slots[ph], slots[1 - ph]
        row0 = ph * R
        xt = x_next_ref[...].astype(_BF16)

        def project_cols(c0, xt=xt, nxt=nxt):
            nxt[:, c0:c0 + PROJ_COL_CHUNK] = jnp.dot(xt, w_ref[:, c0:c0 + PROJ_COL_CHUNK],
                                                     preferred_element_type=_F32)

        def lru_front(c, cur=cur):
            xr = cur[c * SUB:(c + 1) * SUB, XR0:XR0 + C]
            xbuf[PAD + c * SUB:PAD + (c + 1) * SUB, :] = xr
            y = cb_ref[...] + cw_ref[CONV_WIDTH - 1:CONV_WIDTH, :] * xr
            for d in range(1, CONV_WIDTH):
                y = y + (cw_ref[CONV_WIDTH - 1 - d:CONV_WIDTH - d, :]
                         * xbuf[PAD - d + c * SUB:PAD - d + (c + 1) * SUB, :])
            if c == R // SUB - 1:
                xbuf[0:PAD, :] = xr[SUB - PAD:SUB, :]
            yb = y.astype(_BF16)

            def gate_lin(w3_ref, b_ref):
                parts = [jnp.dot(yb[:, s * half:(s + 1) * half], w3_ref[s],
                                 preferred_element_type=_F32) for s in range(2)]
                return jnp.concatenate(parts, axis=1) + b_ref[...]

            r = sigmoid(gate_lin(wrg_ref, brg_ref))
            ig = sigmoid(gate_lin(wig_ref, big_ref))
            lam = lam_ref[...]
            softplus_neg_lam = jnp.maximum(-lam, 0.0) + jnp.log(1.0 + jnp.exp(-jnp.abs(lam)))
            a = jnp.exp(r * ((-LRU_C) * softplus_neg_lam))
            v1 = 1.0 - a * a
            lru_state[c] = (a, jnp.where(v1 > 0.0, v1 * lax.rsqrt(v1), 0.0) * (ig * y))

        def lru_back(c, cur=cur, row0=row0):
            a, u = lru_state.pop(c)
            carry = lru_state["carry"]
            h_groups = []
            for g in range(SUB // SUBLANES):
                ag = a[g * SUBLANES:(g + 1) * SUBLANES, :]
                ug = u[g * SUBLANES:(g + 1) * SUBLANES, :]
                for d in (1, 2, 4):
                    a_sh = pltpu.roll(ag, d, 0)
                    u_sh = pltpu.roll(ug, d, 0)
                    m = row >= d
                    ug = jnp.where(m, ag * u_sh + ug, ug)
                    ag = jnp.where(m, ag * a_sh, ag)
                hg = ag * carry + ug
                h_groups.append(hg)
                carry = jnp.broadcast_to(hg[SUBLANES - 1:SUBLANES, :], (SUBLANES, C))
            lru_state["carry"] = carry
            gr = cur[c * SUB:(c + 1) * SUB, GR0:GR0 + C]
            gelu = 0.5 * gr * (1.0 + jnp.tanh(0.7978845608028654 * (gr + 0.044715 * (gr * gr * gr))))
            lru_ref[row0 + c * SUB:row0 + (c + 1) * SUB, :] = (
                jnp.concatenate(h_groups, axis=0) * gelu).astype(lru_ref.dtype)

        def head_pair(c0, pair, cur=cur):
            two = cur[:, c0 + pair * LANES:c0 + (pair + 1) * LANES]
            return ((pair * HEADS_PER_LANE_TILE, two),
                    (pair * HEADS_PER_LANE_TILE + 1, pltpu.roll(two, HEAD_DIM, 1)))

        def q_epilogue(pair, head_pair=head_pair, row0=row0):
            for h, qh in head_pair(Q0, pair):
                q_ref[row0:row0 + R, h * LANES:(h + 1) * LANES] = jnp.where(
                    data_lane, qh * (HEAD_DIM ** -0.5 * LOG2E), qx_ref[...]).astype(_BF16)

        def k_epilogue(pair, head_pair=head_pair, row0=row0, ph=ph):
            for h, kh in head_pair(K0, pair):
                cols = slice(h * LANES, (h + 1) * LANES)
                k_ref[row0:row0 + R, cols] = jnp.where(data_lane, kh.astype(_BF16),
                                                       kx_ref[row0:row0 + R, cols])
                kpad = jnp.where(data_lane, kh, 0.0)
                for s in range(R // MOBA_BLOCK):
                    km_ref[0, ph * (R // MOBA_BLOCK) + s, :, cols] = jnp.mean(
                        kpad[s * MOBA_BLOCK:(s + 1) * MOBA_BLOCK], axis=0, keepdims=True)

        def v_epilogue(pair, cur=cur, row0=row0):
            vt = cur[:, V0 + pair * LANES:V0 + (pair + 1) * LANES].T.astype(_BF16)
            for hh in range(HEADS_PER_LANE_TILE):
                r0 = (pair * HEADS_PER_LANE_TILE + hh) * VT_ROWS
                vt_ref[0, r0:r0 + HEAD_DIM, row0:row0 + R] = vt[hh * HEAD_DIM:(hh + 1) * HEAD_DIM, :]
                vt_ref[0, r0 + HEAD_DIM:r0 + VT_ROWS, row0:row0 + R] = ones_rows

        mxu_pieces = [(2 * PROJ_COL_CHUNK, functools.partial(project_cols, c0))
                      for c0 in range(0, nxt.shape[1], PROJ_COL_CHUNK)]
        vpu_pieces = []
        for c in range(R // SUB):
            vpu_pieces += [(6 * SUB, functools.partial(lru_front, c)), (4 * SUB, functools.partial(lru_back, c))]
        for pair in range(N_HEAD_PAIRS):
            vpu_pieces += [(80, functools.partial(q_epilogue, pair)),
                           (110, functools.partial(k_epilogue, pair)),
                           (40, functools.partial(v_epilogue, pair))]
        interleave(mxu_pieces, vpu_pieces)
    hcar[...] = lru_state["carry"]


def _proj_lru(x, w_in, qx, kx, conv_w, conv_b, wrg, b_rg, wig, b_ig, lam):
    B, S, D = x.shape
    C = LRU_WIDTH
    nb = S // MOBA_BLOCK
    R = PROJ_ROWS
    step_rows = PROJ_TILES_PER_STEP * R
    n_tiles = B * S // R
    steps_per_seq = S // step_rows
    assert S % step_rows == 0 and PROJ_TILES_PER_STEP == 2
    x2 = x.reshape(B * S, D)
    resident = lambda shape: pl.BlockSpec(shape, lambda i: (0,) * len(shape),
                                          pipeline_mode=pl.Buffered(1))
    tile = lambda offset: pl.BlockSpec(
        (R, D), lambda i: (jnp.minimum(PROJ_TILES_PER_STEP * i + offset, n_tiles - 1), 0))
    q, k, vt, km, lru = pl.pallas_call(
        functools.partial(_proj_kernel, steps_per_seq=steps_per_seq),
        grid=(n_tiles // PROJ_TILES_PER_STEP,),
        in_specs=[
            pl.BlockSpec((R, D), lambda i: (0, 0), pipeline_mode=pl.Buffered(1)),
            tile(1),
            tile(2),
            resident((D, 3 * ATTN_WIDTH + 2 * C)),
            resident((1, LANES)),
            pl.BlockSpec((step_rows, AUG_WIDTH), lambda i: (i % steps_per_seq, 0)),
            resident((CONV_WIDTH, C)),
            resident((1, C)),
            resident((2, C // 2, C // 2)),
            resident((1, C)),
            resident((2, C // 2, C // 2)),
            resident((1, C)),
            resident((1, C)),
        ],
        out_specs=[
            pl.BlockSpec((step_rows, AUG_WIDTH), lambda i: (i, 0)),
            pl.BlockSpec((step_rows, AUG_WIDTH), lambda i: (i, 0)),
            pl.BlockSpec((1, N_HEADS * VT_ROWS, step_rows),
                         lambda i: (i // steps_per_seq, 0, i % steps_per_seq)),
            pl.BlockSpec((1, step_rows // MOBA_BLOCK, 1, AUG_WIDTH),
                         lambda i: (i // steps_per_seq, i % steps_per_seq, 0, 0)),
            pl.BlockSpec((step_rows, C), lambda i: (i, 0)),
        ],
        out_shape=[
            jax.ShapeDtypeStruct((B * S, AUG_WIDTH), _BF16),
            jax.ShapeDtypeStruct((B * S, AUG_WIDTH), _BF16),
            jax.ShapeDtypeStruct((B, N_HEADS * VT_ROWS, S), _BF16),
            jax.ShapeDtypeStruct((B, nb, 1, AUG_WIDTH), _F32),
            jax.ShapeDtypeStruct((B * S, C), _BF16),
        ],
        scratch_shapes=[
            pltpu.VMEM((D, 3 * ATTN_WIDTH + 2 * C), _BF16),
            pltpu.VMEM((R, 3 * ATTN_WIDTH + 2 * C), _F32),
            pltpu.VMEM((R, 3 * ATTN_WIDTH + 2 * C), _F32),
            pltpu.VMEM((R + SUBLANES, C), _F32),
            pltpu.VMEM((SUBLANES, C), _F32),
        ],
        compiler_params=pltpu.CompilerParams(
            dimension_semantics=("arbitrary",),
            vmem_limit_bytes=VMEM_LIMIT_BYTES),
        name="proj_lru",
    )(x2, x2, x2, w_in, qx, kx, conv_w, conv_b, wrg, b_rg, wig, b_ig, lam)
    return (q.reshape(B, S, AUG_WIDTH), k.reshape(B, S, AUG_WIDTH), vt, km, lru)


def _aug_constants(seq_len):
    nb = seq_len // MOBA_BLOCK
    alibi_lane0 = SEL_LANE0 + nb
    lane = np.arange(LANES)
    qx = ((lane >= alibi_lane0) & (lane < alibi_lane0 + ALIBI_PIECES)).astype(np.float32).reshape(1, LANES)

    slopes = np.exp2(-8.0 * np.arange(1, N_HEADS + 1, dtype=np.float32) / N_HEADS).astype(np.float32)
    pos = np.arange(seq_len)
    onehot = (pos[:, None] // MOBA_BLOCK == np.arange(nb)[None, :]).astype(np.float32)
    val = (np.float32(LOG2E) * slopes)[None, :] * (pos[:, None] - seq_len // 2).astype(np.float32)

    def trunc_bf16(v):
        return (v.view(np.uint32) & np.uint32(0xFFFF0000)).view(np.float32)

    hi = trunc_bf16(val)
    mid = trunc_bf16(val - hi)
    lo = val - hi - mid
    pieces = np.stack([hi, mid, lo], axis=-1)
    extras = np.concatenate([
        np.zeros((seq_len, N_HEADS, HEAD_DIM), np.float32),
        np.broadcast_to(onehot[:, None, :], (seq_len, N_HEADS, nb)),
        pieces,
        np.zeros((seq_len, N_HEADS, LANES - alibi_lane0 - ALIBI_PIECES), np.float32)], axis=-1)
    return jnp.asarray(qx), jnp.asarray(extras.reshape(seq_len, AUG_WIDTH).astype(_BF16))


def _attn_kernel(q_ref, k_ref, vt_ref, km_ref, wo_ref, wup_ref, wdn_ref,
                 o_ref, wo_b_ref, wup_b_ref, wdn_b_ref):
    blk = MOBA_BLOCK
    nb = k_ref.shape[1] // blk

    wo_b_ref[...] = wo_ref[...].astype(_BF16)
    wup_b_ref[...] = wup_ref[...].astype(_BF16)
    wdn_b_ref[...] = wdn_ref[...].astype(_BF16)

    lane = lax.broadcasted_iota(jnp.int32, (blk, LANES), 1)
    sel_lane = (lane >= SEL_LANE0) & (lane < SEL_LANE0 + nb)
    key_row = lax.broadcasted_iota(jnp.int32, (blk, blk), 0)
    qry_col = lax.broadcasted_iota(jnp.int32, (blk, blk), 1)
    causal = key_row <= qry_col
    blk_idx = lax.broadcasted_iota(jnp.int32, (nb, blk), 0)

    def query_tile(j, hh):
        nk = (j + 1) * blk
        cols = slice(hh * LANES, (hh + 1) * LANES)
        qa = q_ref[0, j * blk:nk, cols]
        if j > MOBA_TOPK:
            kmh = km_ref[0, :, 0, cols].astype(_BF16)
            gate = lax.dot_general(kmh, qa, _NT_DIMS, preferred_element_type=_F32)
            rank = jnp.zeros((nb, blk), jnp.int32)
            for n2 in range(j):
                g2 = gate[n2:n2 + 1, :]
                beats = (g2 > gate) | ((g2 == gate) & (n2 < blk_idx))
                rank = rank + beats.astype(jnp.int32)
            keep = ((blk_idx < j) & (rank < MOBA_TOPK)) | (blk_idx >= j)
            selb = jnp.where(keep, 0.0, NEG_INF)
            selb_rows = jnp.concatenate(
                [jnp.zeros((SEL_LANE0, blk), _F32), selb,
                 jnp.zeros((LANES - SEL_LANE0 - nb, blk), _F32)], axis=0)
            qa = jnp.where(sel_lane, selb_rows.T.astype(_BF16), qa)
        return qa

    def scores(j, hh):
        nk = (j + 1) * blk
        cols = slice(hh * LANES, (hh + 1) * LANES)
        z = lax.dot_general(k_ref[0, 0:nk, cols], query_tile(j, hh), _NT_DIMS,
                            preferred_element_type=_F32)
        z_own = jnp.where(causal, z[nk - blk:, :], NEG_INF)
        z = z_own if j == 0 else jnp.concatenate([z[:nk - blk, :], z_own], axis=0)
        m = jnp.max(z, axis=0, keepdims=True)
        return z, m

    def probs(zm):
        z, m = zm
        return jnp.exp2(z - m).astype(_BF16)

    def weighted_values(j, hh, p):
        nk = (j + 1) * blk
        vt = vt_ref[0, hh * VT_ROWS:(hh + 1) * VT_ROWS, 0:nk]
        acc = jnp.dot(vt, p, preferred_element_type=_F32)
        return acc[0:HEAD_DIM, :] / acc[HEAD_DIM:HEAD_DIM + 1, :]

    j_order = list(range(0, nb, 2)) + list(range(nb - 1 - nb % 2, 0, -2))
    items = [(j, hh) for j in j_order for hh in range(HEADS_PER_LANE_TILE)]
    n_items = len(items)
    zm, ps, outs = {}, {}, {}
    for t in range(-2 * ATTN_STAGE_LAG, n_items):
        if t + 2 * ATTN_STAGE_LAG < n_items:
            zm[t + 2 * ATTN_STAGE_LAG] = scores(*items[t + 2 * ATTN_STAGE_LAG])
        if 0 <= t + ATTN_STAGE_LAG < n_items:
            ps[t + ATTN_STAGE_LAG] = probs(zm.pop(t + ATTN_STAGE_LAG))
        if t >= 0:
            j, hh = items[t]
            outs[hh] = weighted_values(j, hh, ps.pop(t))
            if hh == HEADS_PER_LANE_TILE - 1:
                o_ref[0, j * blk:(j + 1) * blk, :] = jnp.concatenate(
                    [outs[h] for h in range(HEADS_PER_LANE_TILE)], axis=0).T.astype(o_ref.dtype)


def _moba_attention(q, k, vt, km, w_out, w_up, w_down):
    B, S, _ = q.shape
    nb = S // MOBA_BLOCK
    pair_w = HEADS_PER_LANE_TILE * LANES
    n_steps = B * N_HEAD_PAIRS

    def row_slab(w):
        rows = w.shape[0] // n_steps
        assert w.shape[0] % n_steps == 0 and rows % 16 == 0
        return pl.BlockSpec((rows, w.shape[1]), lambda b, h: (b * N_HEAD_PAIRS + h, 0))

    weights = (w_out, w_up, w_down)
    return pl.pallas_call(
        _attn_kernel,
        grid=(B, N_HEAD_PAIRS),
        in_specs=[
            pl.BlockSpec((1, S, pair_w), lambda b, h: (b, 0, h)),
            pl.BlockSpec((1, S, pair_w), lambda b, h: (b, 0, h)),
            pl.BlockSpec((1, HEADS_PER_LANE_TILE * VT_ROWS, S), lambda b, h: (b, h, 0)),
            pl.BlockSpec((1, nb, 1, pair_w), lambda b, h: (b, 0, 0, h)),
        ] + [row_slab(w) for w in weights],
        out_specs=[pl.BlockSpec((1, S, LANES), lambda b, h: (b, 0, h))]
        + [row_slab(w) for w in weights],
        out_shape=[jax.ShapeDtypeStruct((B, S, ATTN_WIDTH), _BF16)]
        + [jax.ShapeDtypeStruct(w.shape, _BF16) for w in weights],
        compiler_params=pltpu.CompilerParams(
            dimension_semantics=("parallel", "parallel"),
            vmem_limit_bytes=VMEM_LIMIT_BYTES),
        name="moba_attention",
    )(q, k, vt, km, *weights)


def _layer_norm(y, g, b):
    mu = jnp.mean(y, axis=-1, keepdims=True)
    yc = y - mu
    var = jnp.mean(yc * yc, axis=-1, keepdims=True)
    return yc * lax.rsqrt(var + LN_EPS) * g + b


def _mlp_kernel(x_ref, attn_ref, lru_ref, wo_ref, g1_ref, b1_ref, wup_ref, bup_ref,
                wdn_ref, bdn_ref, g2_ref, b2_ref, o_ref):
    n_sub = MLP_ROWS // MLP_SUB_ROWS
    n_ff = wup_ref.shape[1] // MLP_FF_TILE

    def rows(c):
        return slice(c * MLP_SUB_ROWS, (c + 1) * MLP_SUB_ROWS)

    def out_proj(c):
        mix = jnp.dot(attn_ref[rows(c), :], wo_ref[0:ATTN_WIDTH, :], preferred_element_type=_F32)
        return mix + jnp.dot(lru_ref[rows(c), :], wo_ref[ATTN_WIDTH:, :], preferred_element_type=_F32)

    def norm1(c, mix):
        x1 = _layer_norm(DEEPNORM_ALPHA * x_ref[rows(c), :] + mix, g1_ref[...], b1_ref[...])
        return x1, x1.astype(_BF16)

    def ff_tile(x1b, f):
        cols = slice(f * MLP_FF_TILE, (f + 1) * MLP_FF_TILE)
        h = jnp.dot(x1b, wup_ref[:, cols], preferred_element_type=_F32) + bup_ref[:, cols]
        h = jnp.maximum(h, 0.0)
        return jnp.dot((h * h).astype(_BF16), wdn_ref[cols, :], preferred_element_type=_F32)

    def norm2(c, x1, acc):
        y = DEEPNORM_ALPHA * x1 + acc + bdn_ref[...]
        o_ref[rows(c), :] = _layer_norm(y, g2_ref[...], b2_ref[...])

    x1, x1b = norm1(0, out_proj(0))
    pending = None
    for c in range(n_sub):
        nxt_mix = out_proj(c + 1) if c + 1 < n_sub else None
        nxt = None
        acc = None
        for f in range(n_ff):
            part = ff_tile(x1b, f)
            acc = part if acc is None else acc + part
            if f == 1 and nxt_mix is not None:
                nxt = norm1(c + 1, nxt_mix)
            if f == n_ff // 2 and pending is not None:
                norm2(*pending)
                pending = None
        pending = (c, x1, acc)
        if nxt is not None:
            x1, x1b = nxt
    norm2(*pending)


def _outproj_mlp(x2, attn2, lru2, w_out, g1, b1, w_up, b_up, w_down, b_down, g2, b2):
    M, D = x2.shape
    F = w_up.shape[1]
    tm = MLP_ROWS
    resident = lambda shape: pl.BlockSpec(shape, lambda i: (0, 0), pipeline_mode=pl.Buffered(1))
    return pl.pallas_call(
        _mlp_kernel,
        grid=(M // tm,),
        in_specs=[
            pl.BlockSpec((tm, D), lambda i: (i, 0)),
            pl.BlockSpec((tm, ATTN_WIDTH), lambda i: (i, 0)),
            pl.BlockSpec((tm, LRU_WIDTH), lambda i: (i, 0)),
            resident((D, D)),
            resident((1, D)), resident((1, D)),
            resident((D, F)),
            resident((1, F)),
            resident((F, D)),
            resident((1, D)), resident((1, D)), resident((1, D)),
        ],
        out_specs=pl.BlockSpec((tm, D), lambda i: (i, 0)),
        out_shape=jax.ShapeDtypeStruct((M, D), _F32),
        compiler_params=pltpu.CompilerParams(
            dimension_semantics=("parallel",),
            vmem_limit_bytes=VMEM_LIMIT_BYTES),
        name="outproj_mlp",
    )(x2, attn2, lru2, w_out, g1, b1, w_up, b_up, w_down, b_down, g2, b2)


def _block_diag_halves(w):
    n_half = N_LRU_BLOCKS // 2
    eye = jnp.eye(n_half, dtype=w.dtype)
    wh = w.reshape(2, n_half, LRU_BLOCK, LRU_BLOCK)
    full = jnp.einsum('snde,nm->sndme', wh, eye)
    return full.reshape(2, n_half * LRU_BLOCK, n_half * LRU_BLOCK)


def kernel(x, w_in, conv_w, conv_b, w_rg, b_rg, w_ig, b_ig, lru_lambda, w_out, ln1_g, ln1_b,
           w_up, b_up, w_down, b_down, ln2_g, ln2_b):
    B, S, D = x.shape
    depth = w_in.shape[0]
    row = lambda v: v.reshape(1, -1)
    qx, kx = _aug_constants(S)
    for l in range(depth):
        q, k, vt, km, lru = _proj_lru(
            x, w_in[l], qx, kx, conv_w[l], row(conv_b[l]),
            _block_diag_halves(w_rg[l]).astype(_BF16), row(b_rg[l]),
            _block_diag_halves(w_ig[l]).astype(_BF16), row(b_ig[l]),
            row(lru_lambda[l]))
        attn, w_out_b, w_up_b, w_down_b = _moba_attention(q, k, vt, km, w_out[l], w_up[l], w_down[l])
        out = _outproj_mlp(
            x.reshape(B * S, D), attn.reshape(B * S, ATTN_WIDTH), lru.reshape(B * S, LRU_WIDTH),
            w_out_b, row(ln1_g[l]), row(ln1_b[l]),
            w_up_b, row(b_up[l]), w_down_b, row(b_down[l]),
            row(ln2_g[l]), row(ln2_b[l]))
        x = out.reshape(B, S, D)
    return x
```

```python
import functools

import numpy as np

import jax
import jax.numpy as jnp
from jax import lax
from jax.experimental import pallas as pl
from jax.experimental.pallas import tpu as pltpu

D_MODEL = 1024
ATTN_WIDTH = 512
N_HEADS = 8
HEAD_DIM = 64
LRU_WIDTH = 512
N_LRU_BLOCKS = 8
LRU_BLOCK = 64
CONV_WIDTH = 4
LRU_C = 8.0
MOBA_BLOCK = 256
MOBA_TOPK = 3
D_FF = 4096
LN_EPS = 1e-5
DEEPNORM_ALPHA = 2.0 ** 0.25
NEG_INF = -1e30

SUBLANES = 8
LANES = 128
HEADS_PER_LANE_TILE = LANES // HEAD_DIM
N_HEAD_PAIRS = N_HEADS // HEADS_PER_LANE_TILE
AUG_WIDTH = N_HEADS * LANES
SEL_LANE0 = HEAD_DIM
ALIBI_PIECES = 3
VT_ROWS = HEAD_DIM + 16
LOG2E = 1.4426950408889634

ATTN_STAGE_LAG = 2
PROJ_ROWS = 512
PROJ_TILES_PER_STEP = 2
PROJ_COL_CHUNK = 512
PROJ_SUB_ROWS = 256
MLP_ROWS = 512
MLP_SUB_ROWS = 256
MLP_FF_TILE = 2048
VMEM_LIMIT_BYTES = 56 * 1024 * 1024

_F32 = jnp.float32
_BF16 = jnp.bfloat16
_NT_DIMS = (((1,), (1,)), ((), ()))


def _proj_kernel(x0_ref, xa_ref, xb_ref, w32_ref, qx_ref, kx_ref, cw_ref, cb_ref, wrg_ref, brg_ref,
                 wig_ref, big_ref, lam_ref, q_ref, k_ref, vt_ref, km_ref, lru_ref,
                 w_ref, p_s0, p_s1, xbuf, hcar, *, steps_per_seq):
    i = pl.program_id(0)
    R = PROJ_ROWS
    C = LRU_WIDTH
    SUB = PROJ_SUB_ROWS
    PAD = SUBLANES
    Q0, K0, V0, XR0, GR0 = 0, ATTN_WIDTH, 2 * ATTN_WIDTH, 3 * ATTN_WIDTH, 3 * ATTN_WIDTH + C

    slots = (p_s0, p_s1)

    def project(x_tile_ref, slot_ref, c0, width):
        xt = x_tile_ref[...].astype(_BF16)
        slot_ref[:, c0:c0 + width] = jnp.dot(xt, w_ref[:, c0:c0 + width],
                                             preferred_element_type=_F32)

    @pl.when(i == 0)
    def _():
        w_ref[...] = w32_ref[...].astype(_BF16)
        project(x0_ref, p_s0, 0, p_s0.shape[1])

    @pl.when(i % steps_per_seq == 0)
    def _():
        xbuf[0:PAD, :] = jnp.zeros((PAD, C), _F32)
        hcar[...] = jnp.zeros_like(hcar)

    data_lane = lax.broadcasted_iota(jnp.int32, (R, LANES), 1) < HEAD_DIM
    row = lax.broadcasted_iota(jnp.int32, (SUBLANES, C), 0)
    ones_rows = (lax.broadcasted_iota(jnp.int32, (VT_ROWS - HEAD_DIM, R), 0) == 0).astype(_BF16)
    half = C // 2

    def sigmoid(v):
        return 0.5 * jnp.tanh(0.5 * v) + 0.5

    def interleave(mxu_pieces, vpu_pieces):
        done = [0, 0]
        queues = [list(mxu_pieces), list(vpu_pieces)]
        while queues[0] or queues[1]:
            s = 0 if (queues[0] and (not queues[1] or done[0] <= done[1])) else 1
            cost, thunk = queues[s].pop(0)
            done[s] += cost
            thunk()

    lru_state = {"carry": hcar[...]}
    for ph, x_next_ref in enumerate((xa_ref, xb_ref)):
        cur, nxt = slots[ph], slots[1 - ph]
        row0 = ph * R
        xt = x_next_ref[...].astype(_BF16)

        def project_cols(c0, xt=xt, nxt=nxt):
            nxt[:, c0:c0 + PROJ_COL_CHUNK] = jnp.dot(xt, w_ref[:, c0:c0 + PROJ_COL_CHUNK],
                                                     preferred_element_type=_F32)

        def lru_front(c, cur=cur):
            xr = cur[c * SUB:(c + 1) * SUB, XR0:XR0 + C]
            xbuf[PAD + c * SUB:PAD + (c + 1) * SUB, :] = xr
            y = cb_ref[...] + cw_ref[CONV_WIDTH - 1:CONV_WIDTH, :] * xr
            for d in range(1, CONV_WIDTH):
                y = y + (cw_ref[CONV_WIDTH - 1 - d:CONV_WIDTH - d, :]
                         * xbuf[PAD - d + c * SUB:PAD - d + (c + 1) * SUB, :])
            if c == R // SUB - 1:
                xbuf[0:PAD, :] = xr[SUB - PAD:SUB, :]
            yb = y.astype(_BF16)

            def gate_lin(w3_ref, b_ref):
                parts = [jnp.dot(yb[:, s * half:(s + 1) * half], w3_ref[s],
                                 preferred_element_type=_F32) for s in range(2)]
                return jnp.concatenate(parts, axis=1) + b_ref[...]

            r = sigmoid(gate_lin(wrg_ref, brg_ref))
            ig = sigmoid(gate_lin(wig_ref, big_ref))
            lam = lam_ref[...]
            softplus_neg_lam = jnp.maximum(-lam, 0.0) + jnp.log(1.0 + jnp.exp(-jnp.abs(lam)))
            a = jnp.exp(r * ((-LRU_C) * softplus_neg_lam))
            v1 = 1.0 - a * a
            lru_state[c] = (a, jnp.where(v1 > 0.0, v1 * lax.rsqrt(v1), 0.0) * (ig * y))

        def lru_back(c, cur=cur, row0=row0):
            a, u = lru_state.pop(c)
            carry = lru_state["carry"]
            h_groups = []
            for g in range(SUB // SUBLANES):
                ag = a[g * SUBLANES:(g + 1) * SUBLANES, :]
                ug = u[g * SUBLANES:(g + 1) * SUBLANES, :]
                for d in (1, 2, 4):
                    a_sh = pltpu.roll(ag, d, 0)
                    u_sh = pltpu.roll(ug, d, 0)
                    m = row >= d
                    ug = jnp.where(m, ag * u_sh + ug, ug)
                    ag = jnp.where(m, ag * a_sh, ag)
                hg = ag * carry + ug
                h_groups.append(hg)
                carry = jnp.broadcast_to(hg[SUBLANES - 1:SUBLANES, :], (SUBLANES, C))
            lru_state["carry"] = carry
            gr = cur[c * SUB:(c + 1) * SUB, GR0:GR0 + C]
            gelu = 0.5 * gr * (1.0 + jnp.tanh(0.7978845608028654 * (gr + 0.044715 * (gr * gr * gr))))
            lru_ref[row0 + c * SUB:row0 + (c + 1) * SUB, :] = (
                jnp.concatenate(h_groups, axis=0) * gelu).astype(lru_ref.dtype)

        def head_pair(c0, pair, cur=cur):
            two = cur[:, c0 + pair * LANES:c0 + (pair + 1) * LANES]
            return ((pair * HEADS_PER_LANE_TILE, two),
                    (pair * HEADS_PER_LANE_TILE + 1, pltpu.roll(two, HEAD_DIM, 1)))

        def q_epilogue(pair, head_pair=head_pair, row0=row0):
            for h, qh in head_pair(Q0, pair):
                q_ref[row0:row0 + R, h * LANES:(h + 1) * LANES] = jnp.where(
                    data_lane, qh * (HEAD_DIM ** -0.5 * LOG2E), qx_ref[...]).astype(_BF16)

        def k_epilogue(pair, head_pair=head_pair, row0=row0, ph=ph):
            for h, kh in head_pair(K0, pair):
                cols = slice(h * LANES, (h + 1) * LANES)
                k_ref[row0:row0 + R, cols] = jnp.where(data_lane, kh.astype(_BF16),
                                                       kx_ref[row0:row0 + R, cols])
                for s in range(R // MOBA_BLOCK):
                    km = jnp.mean(kh[s * MOBA_BLOCK:(s + 1) * MOBA_BLOCK], axis=0, keepdims=True)
                    km_ref[0, ph * (R // MOBA_BLOCK) + s, :, cols] = jnp.where(data_lane[0:1], km, 0.0)

        def v_epilogue(pair, cur=cur, row0=row0):
            vt = cur[:, V0 + pair * LANES:V0 + (pair + 1) * LANES].T.astype(_BF16)
            for hh in range(HEADS_PER_LANE_TILE):
                r0 = (pair * HEADS_PER_LANE_TILE + hh) * VT_ROWS
                vt_ref[0, r0:r0 + HEAD_DIM, row0:row0 + R] = vt[hh * HEAD_DIM:(hh + 1) * HEAD_DIM, :]
                vt_ref[0, r0 + HEAD_DIM:r0 + VT_ROWS, row0:row0 + R] = ones_rows

        mxu_pieces = [(2 * PROJ_COL_CHUNK, functools.partial(project_cols, c0))
                      for c0 in range(0, nxt.shape[1], PROJ_COL_CHUNK)]
        vpu_pieces = []
        for c in range(R // SUB):
            vpu_pieces += [(6 * SUB, functools.partial(lru_front, c)), (4 * SUB, functools.partial(lru_back, c))]
        for pair in range(N_HEAD_PAIRS):
            vpu_pieces += [(80, functools.partial(q_epilogue, pair)),
                           (110, functools.partial(k_epilogue, pair)),
                           (40, functools.partial(v_epilogue, pair))]
        interleave(mxu_pieces, vpu_pieces)
    hcar[...] = lru_state["carry"]


def _proj_lru(x, w_in, qx, kx, conv_w, conv_b, wrg, b_rg, wig, b_ig, lam):
    B, S, D = x.shape
    C = LRU_WIDTH
    nb = S // MOBA_BLOCK
    R = PROJ_ROWS
    step_rows = PROJ_TILES_PER_STEP * R
    n_tiles = B * S // R
    steps_per_seq = S // step_rows
    assert S % step_rows == 0 and PROJ_TILES_PER_STEP == 2
    x2 = x.reshape(B * S, D)
    resident = lambda shape: pl.BlockSpec(shape, lambda i: (0,) * len(shape),
                                          pipeline_mode=pl.Buffered(1))
    tile = lambda offset: pl.BlockSpec(
        (R, D), lambda i: (jnp.minimum(PROJ_TILES_PER_STEP * i + offset, n_tiles - 1), 0))
    q, k, vt, km, lru = pl.pallas_call(
        functools.partial(_proj_kernel, steps_per_seq=steps_per_seq),
        grid=(n_tiles // PROJ_TILES_PER_STEP,),
        in_specs=[
            pl.BlockSpec((R, D), lambda i: (0, 0), pipeline_mode=pl.Buffered(1)),
            tile(1),
            tile(2),
            resident((D, 3 * ATTN_WIDTH + 2 * C)),
            resident((1, LANES)),
            pl.BlockSpec((step_rows, AUG_WIDTH), lambda i: (i % steps_per_seq, 0)),
            resident((CONV_WIDTH, C)),
            resident((1, C)),
            resident((2, C // 2, C // 2)),
            resident((1, C)),
            resident((2, C // 2, C // 2)),
            resident((1, C)),
            resident((1, C)),
        ],
        out_specs=[
            pl.BlockSpec((step_rows, AUG_WIDTH), lambda i: (i, 0)),
            pl.BlockSpec((step_rows, AUG_WIDTH), lambda i: (i, 0)),
            pl.BlockSpec((1, N_HEADS * VT_ROWS, step_rows),
                         lambda i: (i // steps_per_seq, 0, i % steps_per_seq)),
            pl.BlockSpec((1, step_rows // MOBA_BLOCK, 1, AUG_WIDTH),
                         lambda i: (i // steps_per_seq, i % steps_per_seq, 0, 0)),
            pl.BlockSpec((step_rows, C), lambda i: (i, 0)),
        ],
        out_shape=[
            jax.ShapeDtypeStruct((B * S, AUG_WIDTH), _BF16),
            jax.ShapeDtypeStruct((B * S, AUG_WIDTH), _BF16),
            jax.ShapeDtypeStruct((B, N_HEADS * VT_ROWS, S), _BF16),
            jax.ShapeDtypeStruct((B, nb, 1, AUG_WIDTH), _F32),
            jax.ShapeDtypeStruct((B * S, C), _BF16),
        ],
        scratch_shapes=[
            pltpu.VMEM((D, 3 * ATTN_WIDTH + 2 * C), _BF16),
            pltpu.VMEM((R, 3 * ATTN_WIDTH + 2 * C), _F32),
            pltpu.VMEM((R, 3 * ATTN_WIDTH + 2 * C), _F32),
            pltpu.VMEM((R + SUBLANES, C), _F32),
            pltpu.VMEM((SUBLANES, C), _F32),
        ],
        compiler_params=pltpu.CompilerParams(
            dimension_semantics=("arbitrary",),
            vmem_limit_bytes=VMEM_LIMIT_BYTES),
        name="proj_lru",
    )(x2, x2, x2, w_in, qx, kx, conv_w, conv_b, wrg, b_rg, wig, b_ig, lam)
    return (q.reshape(B, S, AUG_WIDTH), k.reshape(B, S, AUG_WIDTH), vt, km, lru)


def _aug_constants(seq_len):
    nb = seq_len // MOBA_BLOCK
    alibi_lane0 = SEL_LANE0 + nb
    lane = np.arange(LANES)
    qx = ((lane >= alibi_lane0) & (lane < alibi_lane0 + ALIBI_PIECES)).astype(np.float32).reshape(1, LANES)

    slopes = np.exp2(-8.0 * np.arange(1, N_HEADS + 1, dtype=np.float32) / N_HEADS).astype(np.float32)
    pos = np.arange(seq_len)
    onehot = (pos[:, None] // MOBA_BLOCK == np.arange(nb)[None, :]).astype(np.float32)
    val = (np.float32(LOG2E) * slopes)[None, :] * (pos[:, None] - seq_len // 2).astype(np.float32)

    def trunc_bf16(v):
        return (v.view(np.uint32) & np.uint32(0xFFFF0000)).view(np.float32)

    hi = trunc_bf16(val)
    mid = trunc_bf16(val - hi)
    lo = val - hi - mid
    pieces = np.stack([hi, mid, lo], axis=-1)
    extras = np.concatenate([
        np.zeros((seq_len, N_HEADS, HEAD_DIM), np.float32),
        np.broadcast_to(onehot[:, None, :], (seq_len, N_HEADS, nb)),
        pieces,
        np.zeros((seq_len, N_HEADS, LANES - alibi_lane0 - ALIBI_PIECES), np.float32)], axis=-1)
    return jnp.asarray(qx), jnp.asarray(extras.reshape(seq_len, AUG_WIDTH).astype(_BF16))


def _attn_kernel(q_ref, k_ref, vt_ref, km_ref, wo_ref, wup_ref, wdn_ref,
                 o_ref, wo_b_ref, wup_b_ref, wdn_b_ref):
    blk = MOBA_BLOCK
    nb = k_ref.shape[1] // blk

    wo_b_ref[...] = wo_ref[...].astype(_BF16)
    wup_b_ref[...] = wup_ref[...].astype(_BF16)
    wdn_b_ref[...] = wdn_ref[...].astype(_BF16)

    lane = lax.broadcasted_iota(jnp.int32, (blk, LANES), 1)
    sel_lane = (lane >= SEL_LANE0) & (lane < SEL_LANE0 + nb)
    key_row = lax.broadcasted_iota(jnp.int32, (blk, blk), 0)
    qry_col = lax.broadcasted_iota(jnp.int32, (blk, blk), 1)
    causal = key_row <= qry_col
    blk_idx = lax.broadcasted_iota(jnp.int32, (nb, blk), 0)

    def query_tile(j, hh):
        nk = (j + 1) * blk
        cols = slice(hh * LANES, (hh + 1) * LANES)
        qa = q_ref[0, j * blk:nk, cols]
        if j > MOBA_TOPK:
            kmh = km_ref[0, :, 0, cols].astype(_BF16)
            gate = lax.dot_general(kmh, qa, _NT_DIMS, preferred_element_type=_F32)
            rank = jnp.zeros((nb, blk), jnp.int32)
            for n2 in range(j):
                g2 = gate[n2:n2 + 1, :]
                beats = (g2 > gate) | ((g2 == gate) & (n2 < blk_idx))
                rank = rank + beats.astype(jnp.int32)
            keep = ((blk_idx < j) & (rank < MOBA_TOPK)) | (blk_idx >= j)
            selb = jnp.where(keep, 0.0, NEG_INF)
            selb_rows = jnp.concatenate(
                [jnp.zeros((SEL_LANE0, blk), _F32), selb,
                 jnp.zeros((LANES - SEL_LANE0 - nb, blk), _F32)], axis=0)
            qa = jnp.where(sel_lane, selb_rows.T.astype(_BF16), qa)
        return qa

    def scores(j, hh):
        nk = (j + 1) * blk
        cols = slice(hh * LANES, (hh + 1) * LANES)
        z = lax.dot_general(k_ref[0, 0:nk, cols], query_tile(j, hh), _NT_DIMS,
                            preferred_element_type=_F32)
        z_own = jnp.where(causal, z[nk - blk:, :], NEG_INF)
        z = z_own if j == 0 else jnp.concatenate([z[:nk - blk, :], z_own], axis=0)
        m = jnp.max(z, axis=0, keepdims=True)
        return z, m

    def probs(zm):
        z, m = zm
        return jnp.exp2(z - m).astype(_BF16)

    def weighted_values(j, hh, p):
        nk = (j + 1) * blk
        vt = vt_ref[0, hh * VT_ROWS:(hh + 1) * VT_ROWS, 0:nk]
        acc = jnp.dot(vt, p, preferred_element_type=_F32)
        return acc[0:HEAD_DIM, :] / acc[HEAD_DIM:HEAD_DIM + 1, :]

    j_order = list(range(0, nb, 2)) + list(range(nb - 1 - nb % 2, 0, -2))
    items = [(j, hh) for j in j_order for hh in range(HEADS_PER_LANE_TILE)]
    n_items = len(items)
    zm, ps, outs = {}, {}, {}
    for t in range(-2 * ATTN_STAGE_LAG, n_items):
        if t + 2 * ATTN_STAGE_LAG < n_items:
            zm[t + 2 * ATTN_STAGE_LAG] = scores(*items[t + 2 * ATTN_STAGE_LAG])
        if 0 <= t + ATTN_STAGE_LAG < n_items:
            ps[t + ATTN_STAGE_LAG] = probs(zm.pop(t + ATTN_STAGE_LAG))
        if t >= 0:
            j, hh = items[t]
            outs[hh] = weighted_values(j, hh, ps.pop(t))
            if hh == HEADS_PER_LANE_TILE - 1:
                o_ref[0, j * blk:(j + 1) * blk, :] = jnp.concatenate(
                    [outs[h] for h in range(HEADS_PER_LANE_TILE)], axis=0).T.astype(o_ref.dtype)


def _moba_attention(q, k, vt, km, w_out, w_up, w_down):
    B, S, _ = q.shape
    nb = S // MOBA_BLOCK
    pair_w = HEADS_PER_LANE_TILE * LANES
    n_steps = B * N_HEAD_PAIRS

    def row_slab(w):
        rows = w.shape[0] // n_steps
        assert w.shape[0] % n_steps == 0 and rows % 16 == 0
        return pl.BlockSpec((rows, w.shape[1]), lambda b, h: (b * N_HEAD_PAIRS + h, 0))

    weights = (w_out, w_up, w_down)
    return pl.pallas_call(
        _attn_kernel,
        grid=(B, N_HEAD_PAIRS),
        in_specs=[
            pl.BlockSpec((1, S, pair_w), lambda b, h: (b, 0, h)),
            pl.BlockSpec((1, S, pair_w), lambda b, h: (b, 0, h)),
            pl.BlockSpec((1, HEADS_PER_LANE_TILE * VT_ROWS, S), lambda b, h: (b, h, 0)),
            pl.BlockSpec((1, nb, 1, pair_w), lambda b, h: (b, 0, 0, h)),
        ] + [row_slab(w) for w in weights],
        out_specs=[pl.BlockSpec((1, S, LANES), lambda b, h: (b, 0, h))]
        + [row_slab(w) for w in weights],
        out_shape=[jax.ShapeDtypeStruct((B, S, ATTN_WIDTH), _BF16)]
        + [jax.ShapeDtypeStruct(w.shape, _BF16) for w in weights],
        compiler_params=pltpu.CompilerParams(
            dimension_semantics=("parallel", "parallel"),
            vmem_limit_bytes=VMEM_LIMIT_BYTES),
        name="moba_attention",
    )(q, k, vt, km, *weights)


def _layer_norm(y, g, b):
    mu = jnp.mean(y, axis=-1, keepdims=True)
    yc = y - mu
    var = jnp.mean(yc * yc, axis=-1, keepdims=True)
    return yc * lax.rsqrt(var + LN_EPS) * g + b


def _mlp_kernel(x_ref, attn_ref, lru_ref, wo_ref, g1_ref, b1_ref, wup_ref, bup_ref,
                wdn_ref, bdn_ref, g2_ref, b2_ref, o_ref):
    n_sub = MLP_ROWS // MLP_SUB_ROWS
    n_ff = wup_ref.shape[1] // MLP_FF_TILE

    def rows(c):
        return slice(c * MLP_SUB_ROWS, (c + 1) * MLP_SUB_ROWS)

    def out_proj(c):
        mix = jnp.dot(attn_ref[rows(c), :], wo_ref[0:ATTN_WIDTH, :], preferred_element_type=_F32)
        return mix + jnp.dot(lru_ref[rows(c), :], wo_ref[ATTN_WIDTH:, :], preferred_element_type=_F32)

    def norm1(c, mix):
        x1 = _layer_norm(DEEPNORM_ALPHA * x_ref[rows(c), :] + mix, g1_ref[...], b1_ref[...])
        return x1, x1.astype(_BF16)

    def ff_tile(x1b, f):
        cols = slice(f * MLP_FF_TILE, (f + 1) * MLP_FF_TILE)
        h = jnp.dot(x1b, wup_ref[:, cols], preferred_element_type=_F32) + bup_ref[:, cols]
        h = jnp.maximum(h, 0.0)
        return jnp.dot((h * h).astype(_BF16), wdn_ref[cols, :], preferred_element_type=_F32)

    def norm2(c, x1, acc):
        y = DEEPNORM_ALPHA * x1 + acc + bdn_ref[...]
        o_ref[rows(c), :] = _layer_norm(y, g2_ref[...], b2_ref[...])

    x1, x1b = norm1(0, out_proj(0))
    pending = None
    for c in range(n_sub):
        nxt_mix = out_proj(c + 1) if c + 1 < n_sub else None
        nxt = None
        acc = None
        for f in range(n_ff):
            part = ff_tile(x1b, f)
            acc = part if acc is None else acc + part
            if f == 1 and nxt_mix is not None:
                nxt = norm1(c + 1, nxt_mix)
            if f == n_ff // 2 and pending is not None:
                norm2(*pending)
                pending = None
        pending = (c, x1, acc)
        if nxt is not None:
            x1, x1b = nxt
    norm2(*pending)


def _outproj_mlp(x2, attn2, lru2, w_out, g1, b1, w_up, b_up, w_down, b_down, g2, b2):
    M, D = x2.shape
    F = w_up.shape[1]
    tm = MLP_ROWS
    resident = lambda shape: pl.BlockSpec(shape, lambda i: (0, 0), pipeline_mode=pl.Buffered(1))
    return pl.pallas_call(
        _mlp_kernel,
        grid=(M // tm,),
        in_specs=[
            pl.BlockSpec((tm, D), lambda i: (i, 0)),
            pl.BlockSpec((tm, ATTN_WIDTH), lambda i: (i, 0)),
            pl.BlockSpec((tm, LRU_WIDTH), lambda i: (i, 0)),
            resident((D, D)),
            resident((1, D)), resident((1, D)),
            resident((D, F)),
            resident((1, F)),
            resident((F, D)),
            resident((1, D)), resident((1, D)), resident((1, D)),
        ],
        out_specs=pl.BlockSpec((tm, D), lambda i: (i, 0)),
        out_shape=jax.ShapeDtypeStruct((M, D), _F32),
        compiler_params=pltpu.CompilerParams(
            dimension_semantics=("parallel",),
            vmem_limit_bytes=VMEM_LIMIT_BYTES),
        name="outproj_mlp",
    )(x2, attn2, lru2, w_out, g1, b1, w_up, b_up, w_down, b_down, g2, b2)


def _block_diag_halves(w):
    n_half = N_LRU_BLOCKS // 2
    eye = jnp.eye(n_half, dtype=w.dtype)
    wh = w.reshape(2, n_half, LRU_BLOCK, LRU_BLOCK)
    full = jnp.einsum('snde,nm->sndme', wh, eye)
    return full.reshape(2, n_half * LRU_BLOCK, n_half * LRU_BLOCK)


def kernel(x, w_in, conv_w, conv_b, w_rg, b_rg, w_ig, b_ig, lru_lambda, w_out, ln1_g, ln1_b,
           w_up, b_up, w_down, b_down, ln2_g, ln2_b):
    B, S, D = x.shape
    depth = w_in.shape[0]
    row = lambda v: v.reshape(1, -1)
    qx, kx = _aug_constants(S)
    for l in range(depth):
        q, k, vt, km, lru = _proj_lru(
            x, w_in[l], qx, kx, conv_w[l], row(conv_b[l]),
            _block_diag_halves(w_rg[l]).astype(_BF16), row(b_rg[l]),
            _block_diag_halves(w_ig[l]).astype(_BF16), row(b_ig[l]),
            row(lru_lambda[l]))
        attn, w_out_b, w_up_b, w_down_b = _moba_attention(q, k, vt, km, w_out[l], w_up[l], w_down[l])
        out = _outproj_mlp(
            x.reshape(B * S, D), attn.reshape(B * S, ATTN_WIDTH), lru.reshape(B * S, LRU_WIDTH),
            w_out_b, row(ln1_g[l]), row(ln1_b[l]),
            w_up_b, row(b_up[l]), w_down_b, row(b_down[l]),
            row(ln2_g[l]), row(ln2_b[l]))
        x = out.reshape(B, S, D)
    return x
```
